```python
import math
import jax, jax.numpy as jnp
from jax import lax
import numpy as np

D_MODEL = 2048
BATCH = 2
SEQ = 8192
DEPTH = 4
DEC_BATCH = 1
DEC_SEQ = 8192
PAST_LEN = 128

HG_HEADS = 8
HG_DK = 128
HG_DV = 128
HG_WIDTH = HG_HEADS * HG_DV
HG_CHUNK = 64
MLA_HEADS = 8
QK_NOPE = 128
QK_ROPE = 64
V_DIM = 128
Q_LORA = 512
KV_LORA = 512
MLA_WIDTH = MLA_HEADS * V_DIM
Q_BLOCK = 128
ROPE_THETA = 10000.0
D_MIX = HG_WIDTH + MLA_WIDTH
OFF_Q = 0
OFF_FF = OFF_Q + HG_HEADS * HG_DK
OFF_FB = OFF_FF + HG_HEADS * HG_DK
OFF_I = OFF_FB + HG_HEADS * HG_DK
OFF_G = OFF_I + HG_WIDTH
OFF_QA = OFF_G + HG_WIDTH
OFF_KVA = OFF_QA + Q_LORA
OFF_KR = OFF_KVA + KV_LORA
IN_COLS = OFF_KR + QK_ROPE
D_FF = 5632
CONV_W = 3
PLE_DIM = 256
EPS = 1e-6

kernel_name = "hybrid_hgrn2_mla_encoder"


def rms_norm(x, g):
    xf = x.astype(jnp.float32)
    y = xf * lax.rsqrt(jnp.mean(xf * xf, axis=-1, keepdims=True) + EPS)
    return (y * g.astype(jnp.float32)).astype(x.dtype)


def rope(x, pos):
    inv = ROPE_THETA ** (-jnp.arange(0, QK_ROPE, 2, dtype=jnp.float32) / QK_ROPE)
    ang = pos[:, None] * inv[None, :]
    cos = jnp.cos(ang)[None, :, None, :]
    sin = jnp.sin(ang)[None, :, None, :]
    xf = x.astype(jnp.float32)
    x1, x2 = xf[..., : QK_ROPE // 2], xf[..., QK_ROPE // 2:]
    return jnp.concatenate([x1 * cos - x2 * sin, x2 * cos + x1 * sin], axis=-1).astype(x.dtype)


def hgrn2_chunk_scan(q, k, v, log_f):
    B, S, H, DK = q.shape
    DV = v.shape[-1]
    n = S // HG_CHUNK

    def to_chunks(t):
        return t.reshape(B, n, HG_CHUNK, H, t.shape[-1]).transpose(1, 0, 3, 2, 4)

    mask = jnp.tril(jnp.ones((HG_CHUNK, HG_CHUNK), dtype=bool))[:, :, None]

    def step(state, inp):
        qc, kc, vc, lfc = inp
        b = jnp.cumsum(lfc, axis=2)
        o_inter = jnp.einsum('bhcd,bhde->bhce', qc * jnp.exp(b), state)
        diff = b[:, :, :, None, :] - b[:, :, None, :, :]
        decay = jnp.where(mask, jnp.exp(jnp.where(mask, diff, 0.0)), 0.0)
        a = jnp.einsum('bhtd,bhsd,bhtsd->bhts', qc, kc, decay)
        o = o_inter + jnp.einsum('bhts,bhse->bhte', a, vc)
        b_last = b[:, :, -1:, :]
        new_state = jnp.exp(b_last[:, :, 0, :])[..., None] * state + jnp.einsum(
            'bhsd,bhse->bhde', kc * jnp.exp(b_last - b), vc)
        return new_state, o

    s0 = jnp.zeros((B, H, DK, DV), jnp.float32)
    _, o = lax.scan(step, s0, (to_chunks(q), to_chunks(k), to_chunks(v), to_chunks(log_f)))
    return o.transpose(1, 0, 3, 2, 4).reshape(B, S, H, DV)


def hgrn2_bidir(z_q, z_ff, z_fb, z_i, z_g, lb_f, lb_b, gain):
    B, S, _ = z_q.shape
    q = z_q.reshape(B, S, HG_HEADS, HG_DK).astype(jnp.float32)
    v = z_i.reshape(B, S, HG_HEADS, HG_DV).astype(jnp.float32)

    def gates(zf, lb):
        lb = lb.reshape(HG_HEADS, HG_DK).astype(jnp.float32)
        log_f = jnp.logaddexp(jnp.log(lb), jnp.log1p(-lb) + jax.nn.log_sigmoid(
            zf.reshape(B, S, HG_HEADS, HG_DK).astype(jnp.float32)))
        return -jnp.expm1(log_f), log_f

    k_f, lf_f = gates(z_ff, lb_f)
    k_b, lf_b = gates(z_fb, lb_b)
    o_f = hgrn2_chunk_scan(q, k_f, v, lf_f)
    flip = lambda t: jnp.flip(t, axis=1)
    o_b = flip(hgrn2_chunk_scan(flip(q), flip(k_b), flip(v), flip(lf_b)))
    o = o_f + o_b
    o = o * lax.rsqrt(jnp.mean(o * o, axis=-1, keepdims=True) + EPS) * gain.reshape(
        HG_HEADS, HG_DV).astype(jnp.float32)
    out = o.reshape(B, S, HG_WIDTH) * jax.nn.silu(z_g.astype(jnp.float32))
    return out.astype(z_q.dtype)


def mla(z_qa, z_kva, z_kr, q_a_norm, w_q_b, kv_a_norm, w_kv_b):
    B, S, _ = z_qa.shape
    pos = jnp.arange(S, dtype=jnp.float32)
    q = (rms_norm(z_qa, q_a_norm) @ w_q_b).reshape(B, S, MLA_HEADS, QK_NOPE + QK_ROPE)
    q = jnp.concatenate([q[..., :QK_NOPE], rope(q[..., QK_NOPE:], pos)], axis=-1)
    kv = (rms_norm(z_kva, kv_a_norm) @ w_kv_b).reshape(B, S, MLA_HEADS, QK_NOPE + V_DIM)
    k_r = rope(z_kr[:, :, None, :], pos)
    k = jnp.concatenate([kv[..., :QK_NOPE],
                         jnp.broadcast_to(k_r, (B, S, MLA_HEADS, QK_ROPE))], axis=-1)
    v = kv[..., QK_NOPE:]
    scale = (QK_NOPE + QK_ROPE) ** -0.5
    qb = q.reshape(B, S // Q_BLOCK, Q_BLOCK, MLA_HEADS, QK_NOPE + QK_ROPE).transpose(1, 0, 2, 3, 4)

    def attend(q_blk):
        s = jnp.einsum('bqhd,bkhd->bhqk', q_blk, k).astype(jnp.float32) * scale
        p = jax.nn.softmax(s, axis=-1).astype(v.dtype)
        return jnp.einsum('bhqk,bkhd->bqhd', p, v)

    o = lax.map(attend, qb)
    return o.transpose(1, 0, 2, 3, 4).reshape(B, S, MLA_WIDTH)


def conv_ffn(x, w_up, conv_w, conv_b, w_down):
    u = x @ w_up
    gate, up = u[..., :D_FF], u[..., D_FF:]
    gp = jnp.pad(gate, ((0, 0), (1, 1), (0, 0)))
    gate = gp[:, :-2] * conv_w[0] + gp[:, 1:-1] * conv_w[1] + gp[:, 2:] * conv_w[2] + conv_b
    return (jax.nn.silu(gate) * up) @ w_down


def trunk(x, p, attn_norm, w_in, hg_lower, hg_norm, q_a_norm, w_q_b, kv_a_norm, w_kv_b,
          w_out, ffn_norm, w_up, conv_w, conv_b, w_down, ple_norm, w_ple_gate, w_ple_proj,
          final_norm):
    lb_all = jnp.cumsum(jax.nn.softmax(hg_lower.astype(jnp.float32), axis=1), axis=1)
    lb_all = jnp.maximum(lb_all - lb_all[:, :1], 0.0)
    h = x
    for l in range(DEPTH):
        z = rms_norm(h, attn_norm[l]) @ w_in[l]
        o_hg = hgrn2_bidir(z[..., OFF_Q:OFF_FF], z[..., OFF_FF:OFF_FB], z[..., OFF_FB:OFF_I],
                           z[..., OFF_I:OFF_G], z[..., OFF_G:OFF_QA],
                           lb_all[0, l], lb_all[1, l], hg_norm[l])
        o_mla = mla(z[..., OFF_QA:OFF_KVA], z[..., OFF_KVA:OFF_KR], z[..., OFF_KR:IN_COLS],
                    q_a_norm[l], w_q_b[l], kv_a_norm[l], w_kv_b[l])
        h = h + jnp.concatenate([o_hg, o_mla], axis=-1) @ w_out[l]
        h = h + conv_ffn(rms_norm(h, ffn_norm[l]), w_up[l], conv_w[l], conv_b[l], w_down[l])
        h = h + (p[l] @ w_ple_proj[l]) * jax.nn.sigmoid(rms_norm(h, ple_norm[l]) @ w_ple_gate[l])
    return rms_norm(h, final_norm)


def setup_inputs(seed: int = 0) -> dict:
    key = jax.random.key(seed)
    ks = jax.random.split(key, 24)
    f32 = jnp.float32
    nrm = lambda k, shape, s: jax.random.normal(k, shape, f32) * s
    gain = lambda k, shape: 1.0 + 0.05 * jax.random.normal(k, shape, f32)
    return {
        "x_prompt": nrm(ks[0], (BATCH, SEQ, D_MODEL), 1.0),
        "x_sample": nrm(ks[1], (DEC_BATCH, DEC_SEQ, D_MODEL), 1.0),
        "p_prompt": nrm(ks[2], (DEPTH, BATCH, SEQ, PLE_DIM), 1.0),
        "p_sample": nrm(ks[3], (DEPTH, DEC_BATCH, DEC_SEQ, PLE_DIM), 1.0),
        "attn_norm": gain(ks[4], (DEPTH, D_MODEL)),
        "w_in": nrm(ks[5], (DEPTH, D_MODEL, IN_COLS), D_MODEL ** -0.5),
        "hg_lower": nrm(ks[6], (2, DEPTH, HG_HEADS * HG_DK), 0.1),
        "hg_norm": gain(ks[7], (DEPTH, HG_WIDTH)),
        "q_a_norm": gain(ks[8], (DEPTH, Q_LORA)),
        "w_q_b": nrm(ks[9], (DEPTH, Q_LORA, MLA_HEADS * (QK_NOPE + QK_ROPE)), Q_LORA ** -0.5),
        "kv_a_norm": gain(ks[10], (DEPTH, KV_LORA)),
        "w_kv_b": nrm(ks[11], (DEPTH, KV_LORA, MLA_HEADS * (QK_NOPE + V_DIM)), KV_LORA ** -0.5),
        "w_out": nrm(ks[12], (DEPTH, D_MIX, D_MODEL), D_MIX ** -0.5),
        "ffn_norm": gain(ks[13], (DEPTH, D_MODEL)),
        "w_up": nrm(ks[14], (DEPTH, D_MODEL, 2 * D_FF), D_MODEL ** -0.5),
        "conv_w": nrm(ks[15], (DEPTH, CONV_W, D_FF), CONV_W ** -0.5),
        "conv_b": nrm(ks[16], (DEPTH, D_FF), 0.01),
        "w_down": nrm(ks[17], (DEPTH, D_FF, D_MODEL), D_FF ** -0.5),
        "ple_norm": gain(ks[18], (DEPTH, D_MODEL)),
        "w_ple_gate": nrm(ks[19], (DEPTH, D_MODEL, D_MODEL), D_MODEL ** -0.5),
        "w_ple_proj": nrm(ks[20], (DEPTH, PLE_DIM, D_MODEL), PLE_DIM ** -0.5),
        "final_norm": gain(ks[21], (D_MODEL,)),
    }


def reference(x_prompt, x_sample, p_prompt, p_sample, attn_norm, w_in, hg_lower, hg_norm,
              q_a_norm, w_q_b, kv_a_norm, w_kv_b, w_out, ffn_norm, w_up, conv_w, conv_b,
              w_down, ple_norm, w_ple_gate, w_ple_proj, final_norm):
    y_prompt = trunk(x_prompt, p_prompt, attn_norm, w_in, hg_lower, hg_norm, q_a_norm, w_q_b,
                     kv_a_norm, w_kv_b, w_out, ffn_norm, w_up, conv_w, conv_b, w_down,
                     ple_norm, w_ple_gate, w_ple_proj, final_norm)
    y_sample = trunk(x_sample, p_sample, attn_norm, w_in, hg_lower, hg_norm, q_a_norm, w_q_b,
                     kv_a_norm, w_kv_b, w_out, ffn_norm, w_up, conv_w, conv_b, w_down,
                     ple_norm, w_ple_gate, w_ple_proj, final_norm)
    return (y_prompt, y_sample)
```

```python
import functools
import math

import jax
import jax.numpy as jnp
import numpy as np
from jax import lax
from jax.experimental import pallas as pl
from jax.experimental.pallas import tpu as pltpu

F32 = jnp.float32
BF16 = jnp.bfloat16

D_MODEL = 2048
DEPTH = 4
HEADS = 8
HEAD_DIM = 128
ROPE_DIM = 64
LORA = 512
HG_WIDTH = HEADS * HEAD_DIM
OFF_Q, OFF_FF, OFF_FB, OFF_I, OFF_G, OFF_QA, OFF_KVA, OFF_KR = (
    0, 1024, 2048, 3072, 4096, 5120, 5632, 6144)
D_FF = 5632
PLE_DIM = 256
EPS = 1e-6
ROPE_THETA = 10000.0
NEG = -1e30

LANES = 128
SUBLANES = 8
VMEM_LIMIT = 56 * 2**20

HG_CHUNK = 64
HG_ROWS = SUBLANES
HG_STEP = 512
Q_PAD = 2 * LANES
QK_SCALE = (HEAD_DIM + ROPE_DIM) ** -0.5 * math.log2(math.e)


def _params(*sem):
    return pltpu.CompilerParams(dimension_semantics=sem, vmem_limit_bytes=VMEM_LIMIT)


def _pick(n, *cands):
    for c in cands:
        if n % c == 0:
            return c
    raise ValueError(f"no tile for {n} in {cands}")


def _rms(x, g):
    ms = jnp.mean(x * x, axis=-1, keepdims=True)
    return x * lax.rsqrt(ms + EPS) * g


def _sigmoid(x):
    return 1.0 / (1.0 + jnp.exp(-x))


def _dot(a, b):
    return jnp.dot(a, b, preferred_element_type=F32)


def _dot_nt(a, b):
    return lax.dot_general(a, b, (((1,), (1,)), ((), ())), preferred_element_type=F32)


def _dot_tn(a, b):
    return lax.dot_general(a, b, (((0,), (0,)), ((), ())), preferred_element_type=F32)


def _norm_mm_kernel(x_ref, g_ref, w_ref, o_ref, xn_ref):
    @pl.when(pl.program_id(1) == 0)
    def _():
        xn_ref[...] = _rms(x_ref[...], g_ref[...]).astype(BF16)

    o_ref[...] = _dot(xn_ref[...], w_ref[...]).astype(o_ref.dtype)


def _norm_mm(x, g, w, tm, tn):
    m, k = x.shape
    n = w.shape[1]
    return pl.pallas_call(
        _norm_mm_kernel,
        grid=(m // tm, n // tn),
        in_specs=[
            pl.BlockSpec((tm, k), lambda i, j: (i, 0)),
            pl.BlockSpec((1, k), lambda i, j: (0, 0)),
            pl.BlockSpec((k, tn), lambda i, j: (0, j)),
        ],
        out_specs=pl.BlockSpec((tm, tn), lambda i, j: (i, j)),
        out_shape=jax.ShapeDtypeStruct((m, n), F32),
        scratch_shapes=[pltpu.VMEM((tm, k), BF16)],
        compiler_params=_params("parallel", "arbitrary"),
        name="norm_mm",
    )(x, g.reshape(1, k), w)


def _hgrn_gates(z, loglb, log1m):
    l1p = jnp.log(1.0 + jnp.exp(-jnp.abs(z)))
    c = log1m + (jnp.minimum(z, 0.0) - l1p)
    logf = jnp.maximum(loglb, c) + jnp.log(1.0 + jnp.exp(-jnp.abs(loglb - c)))
    logk = log1m + (jnp.minimum(-z, 0.0) - l1p)
    return logf, logk


def _hgrn_chunk(q, zf, v, cmat, lvl, emat, loglb, log1m, st_ref, reverse):
    c = q.shape[0]
    nv = c // HG_ROWS
    logf, logk = _hgrn_gates(zf, loglb, log1m)

    hi = logf.astype(BF16)
    r1 = logf - hi.astype(F32)
    mid = r1.astype(BF16)
    lo = (r1 - mid.astype(F32)).astype(BF16)
    bb = _dot(cmat, jnp.concatenate([hi, mid, lo], axis=1))
    b = bb[:, :LANES] + bb[:, LANES:2 * LANES] + bb[:, 2 * LANES:]

    def rows(x, p):
        i = nv - 1 - p if reverse else p
        return x[i * HG_ROWS:(i + 1) * HG_ROWS, :]

    def assemble(groups):
        return jnp.concatenate(groups[::-1] if reverse else groups, axis=0)

    edge = 0 if reverse else HG_ROWS - 1
    b_p = [rows(b, p) for p in range(nv)]
    q_p = [rows(q, p) for p in range(nv)]
    lk_p = [rows(logk, p) for p in range(nv)]
    bnd = [jnp.broadcast_to(x[edge:edge + 1, :], (HG_ROWS, LANES)) for x in b_p]
    zero = jnp.zeros((HG_ROWS, LANES), F32)

    a = jnp.zeros((c, c), F32)
    level, mv = 1, nv // 2
    while mv >= 1:
        qm = [zero] * nv
        km = [zero] * nv
        for g in range(nv // (2 * mv)):
            ref = bnd[2 * mv * g + mv - 1]
            for p in range(2 * mv * g, 2 * mv * g + mv):
                km[p] = jnp.exp(lk_p[p] + (ref - b_p[p]))
            for p in range(2 * mv * g + mv, 2 * mv * (g + 1)):
                qm[p] = q_p[p] * jnp.exp(b_p[p] - ref)
        al = _dot_nt(assemble(qm).astype(BF16), assemble(km).astype(BF16))
        a = jnp.where(lvl == float(level), al, a)
        level += 1
        mv //= 2

    rib = lax.broadcasted_iota(jnp.int32, (HG_ROWS, LANES), 0)
    slabs = []
    for s in range(HG_ROWS):
        valid = (rib <= s) if reverse else (rib >= s)
        col = []
        for p in range(nv):
            cs = jnp.broadcast_to((b_p[p] - lk_p[p])[s:s + 1, :], (HG_ROWS, LANES))
            col.append(q_p[p] * jnp.exp(jnp.where(valid, b_p[p] - cs, NEG)))
        slabs.append(assemble(col).astype(BF16))
    adiag = _dot(jnp.concatenate(slabs, axis=1), emat)
    a = jnp.where(lvl == float(level), adiag, a)

    tot = bnd[nv - 1]
    qbar = assemble([q_p[p] * jnp.exp(b_p[p]) for p in range(nv)]).astype(BF16)
    kbar = assemble([jnp.exp(lk_p[p] + (tot - b_p[p])) for p in range(nv)]).astype(BF16)
    vb = v.astype(BF16)
    st = st_ref[...]
    o = _dot(a.astype(BF16), vb) + _dot_nt(qbar, st.astype(BF16))
    st_ref[...] = jnp.exp(tot[0:1, :]) * st + _dot_tn(vb, kbar)
    return o


def _hgrn_kernel(qf_ref, zf_ref, vf_ref, gf_ref, qb_ref, zb_ref, vb_ref, gb_ref,
                 crow_ref, cmat_ref, lvl_ref, emat_ref, o_ref,
                 part_ref, stf_ref, stb_ref, *, step, chunk, nsteps):
    c = pl.program_id(2)

    @pl.when(c == 0)
    def _():
        stf_ref[...] = jnp.zeros_like(stf_ref)
        stb_ref[...] = jnp.zeros_like(stb_ref)

    crow = crow_ref[0]
    gain = crow[4:5, :]
    emat = emat_ref[...]
    nch = step // chunk
    second = c >= nsteps // 2

    def emit(o, row, g):
        @pl.when(jnp.logical_not(second))
        def _():
            part_ref[pl.ds(row, chunk), :] = o

        @pl.when(second)
        def _():
            t = o + part_ref[pl.ds(row, chunk), :]
            o_ref[pl.ds(row, chunk), :] = (_rms(t, gain) * (g * _sigmoid(g))).astype(BF16)

    def body(j, carry):
        rf = pl.multiple_of(j * chunk, chunk)
        sl = pl.ds(rf, chunk)
        o = _hgrn_chunk(qf_ref[sl, :], zf_ref[sl, :], vf_ref[sl, :], cmat_ref[0], lvl_ref[0], emat,
                        crow[0:1, :], crow[1:2, :], stf_ref, False)
        emit(o, pl.multiple_of(c * step + rf, chunk), gf_ref[sl, :])
        rb = pl.multiple_of((nch - 1 - j) * chunk, chunk)
        sl = pl.ds(rb, chunk)
        o = _hgrn_chunk(qb_ref[sl, :], zb_ref[sl, :], vb_ref[sl, :], cmat_ref[1], lvl_ref[1], emat,
                        crow[2:3, :], crow[3:4, :], stb_ref, True)
        emit(o, pl.multiple_of((nsteps - 1 - c) * step + rb, chunk), gb_ref[sl, :])
        return carry

    lax.fori_loop(0, nch, body, 0)


def _hgrn_consts(chunk):
    t = np.arange(chunk)[:, None]
    s = np.arange(chunk)[None, :]
    cm = np.stack([(s <= t), (s >= t)]).astype(np.float32)
    nlev = int(round(math.log2(chunk // HG_ROWS)))
    lv = np.zeros((chunk, chunk), np.float32)
    for level in range(1, nlev + 1):
        m = chunk >> level
        hit = (t // (2 * m) == s // (2 * m)) & (t // m != s // m) & (s < t)
        lv[hit] = level
    lv[(t // HG_ROWS) == (s // HG_ROWS)] = nlev + 1
    off = (lv >= 1) & (lv <= nlev)
    lvb = np.where(off.T, lv.T, 0.0)
    lvb[(t // HG_ROWS) == (s // HG_ROWS)] = nlev + 1
    lvl = np.stack([lv, lvb]).astype(np.float32)
    em = (np.arange(HG_ROWS * LANES)[:, None] // LANES == (np.arange(chunk)[None, :] % HG_ROWS))
    return jnp.asarray(cm, BF16), jnp.asarray(lvl, F32), jnp.asarray(em.astype(np.float32), BF16)


def _hgrn(z, crow, nb, seq):
    m = z.shape[0]
    step = _pick(seq // 2, HG_STEP, 256, 128, 64)
    chunk = HG_CHUNK
    n = seq // step
    half = n // 2
    cmat, lvl, emat = _hgrn_consts(chunk)

    def zspec(col, fwd, gate=False):
        if fwd:
            blk = (lambda c: jnp.maximum(c, half)) if gate else (lambda c: c)
        else:
            blk = (lambda c: jnp.minimum(n - 1 - c, half - 1)) if gate else (lambda c: n - 1 - c)
        return pl.BlockSpec((step, LANES), lambda b, h, c: (b * n + blk(c), col // LANES + h))

    const3 = lambda shape: pl.BlockSpec(shape, lambda b, h, c: (0, 0, 0))
    return pl.pallas_call(
        functools.partial(_hgrn_kernel, step=step, chunk=chunk, nsteps=n),
        grid=(nb, HEADS, n),
        in_specs=[
            zspec(OFF_Q, True), zspec(OFF_FF, True), zspec(OFF_I, True), zspec(OFF_G, True, True),
            zspec(OFF_Q, False), zspec(OFF_FB, False), zspec(OFF_I, False), zspec(OFF_G, False, True),
            pl.BlockSpec((1, SUBLANES, LANES), lambda b, h, c: (h, 0, 0)),
            const3(cmat.shape), const3(lvl.shape),
            pl.BlockSpec(emat.shape, lambda b, h, c: (0, 0)),
        ],
        out_specs=pl.BlockSpec((seq, LANES), lambda b, h, c: (b, h)),
        out_shape=jax.ShapeDtypeStruct((m, HG_WIDTH), BF16),
        scratch_shapes=[
            pltpu.VMEM((seq, LANES), F32),
            pltpu.VMEM((HEAD_DIM, HEAD_DIM), F32),
            pltpu.VMEM((HEAD_DIM, HEAD_DIM), F32),
        ],
        compiler_params=_params("parallel", "parallel", "arbitrary"),
        name="hgrn2",
    )(z, z, z, z, z, z, z, z, crow, cmat, lvl, emat)


def _qproj_kernel(z_ref, g_ref, w_ref, cos_ref, sin_ref, o_ref):
    xn = _rms(z_ref[...], g_ref[...]).astype(BF16)
    cz = cos_ref[...]
    sz = sin_ref[...]
    for h in range(HEADS):
        acc = _dot(xn, w_ref[:, h * 3 * LANES:(h + 1) * 3 * LANES])
        o_ref[:, h * Q_PAD:h * Q_PAD + LANES] = (acc[:, :LANES] * QK_SCALE).astype(BF16)
        rope = acc[:, LANES:2 * LANES] * cz + acc[:, 2 * LANES:] * sz
        o_ref[:, h * Q_PAD + LANES:(h + 1) * Q_PAD] = (rope * QK_SCALE).astype(BF16)


def _qproj(z, g, w, cosz, sinz, seq):
    m = z.shape[0]
    tm = _pick(seq, 512, 256, 128)
    ns = seq // tm
    return pl.pallas_call(
        _qproj_kernel,
        grid=(m // tm,),
        in_specs=[
            pl.BlockSpec((tm, LORA), lambda i: (i, OFF_QA // LORA)),
            pl.BlockSpec((1, LORA), lambda i: (0, 0)),
            pl.BlockSpec(w.shape, lambda i: (0, 0)),
            pl.BlockSpec((tm, LANES), lambda i: (i % ns, 0)),
            pl.BlockSpec((tm, LANES), lambda i: (i % ns, 0)),
        ],
        out_specs=pl.BlockSpec((tm, HEADS * Q_PAD), lambda i: (i, 0)),
        out_shape=jax.ShapeDtypeStruct((m, HEADS * Q_PAD), BF16),
        compiler_params=_params("parallel"),
        name="mla_q_proj",
    )(z, g.reshape(1, LORA), w, cosz, sinz)


def _kvproj_kernel(z_ref, kr_ref, g_ref, wk_ref, wvt_ref, cos_ref, sin_ref, k_ref, vt_ref):
    xn = _rms(z_ref[...], g_ref[...]).astype(BF16)
    kr = kr_ref[...]
    krope = (kr[:, :LANES] * cos_ref[...] + kr[:, LANES:] * sin_ref[...]).astype(BF16)
    kn = _dot(xn, wk_ref[...])
    for h in range(HEADS):
        k_ref[:, h * Q_PAD:h * Q_PAD + LANES] = kn[:, h * LANES:(h + 1) * LANES].astype(BF16)
        k_ref[:, h * Q_PAD + LANES:(h + 1) * Q_PAD] = krope
    vt_ref[0, 0] = _dot_nt(wvt_ref[...], xn).astype(BF16)


def _kvproj(z, zkr, g, wk, wvt, cosz, sinz, nb, seq, tk):
    m = z.shape[0]
    ns = seq // tk
    return pl.pallas_call(
        _kvproj_kernel,
        grid=(m // tk,),
        in_specs=[
            pl.BlockSpec((tk, LORA), lambda i: (i, OFF_KVA // LORA)),
            pl.BlockSpec((tk, 2 * LANES), lambda i: (i, 0)),
            pl.BlockSpec((1, LORA), lambda i: (0, 0)),
            pl.BlockSpec(wk.shape, lambda i: (0, 0)),
            pl.BlockSpec(wvt.shape, lambda i: (0, 0)),
            pl.BlockSpec((tk, LANES), lambda i: (i % ns, 0)),
            pl.BlockSpec((tk, LANES), lambda i: (i % ns, 0)),
        ],
        out_specs=[
            pl.BlockSpec((tk, HEADS * Q_PAD), lambda i: (i, 0)),
            pl.BlockSpec((1, 1, HG_WIDTH, tk), lambda i: (i // ns, i % ns, 0, 0)),
        ],
        out_shape=[
            jax.ShapeDtypeStruct((m, HEADS * Q_PAD), BF16),
            jax.ShapeDtypeStruct((nb, ns, HG_WIDTH, tk), BF16),
        ],
        compiler_params=_params("parallel"),
        name="mla_kv_proj",
    )(z, zkr, g.reshape(1, LORA), wk, wvt, cosz, sinz)


def _attn_kernel(q_ref, k_ref, vt_ref, o_ref, acc_ref, *, tk, nk):
    q = q_ref[...]
    tq = q.shape[0]
    acc_ref[...] = jnp.zeros_like(acc_ref)

    def body(j, carry):
        m, l = carry
        kb = k_ref[pl.ds(pl.multiple_of(j * tk, tk), tk), :]
        st = _dot_nt(kb, q)
        m_new = jnp.maximum(m, jnp.max(st, axis=0, keepdims=True))
        alpha = jnp.exp2(m - m_new)
        p = jnp.exp2(st - m_new)
        l = alpha * l + jnp.sum(p, axis=0, keepdims=True)
        acc_ref[...] = alpha * acc_ref[...] + _dot(vt_ref[0, j], p.astype(BF16))
        return m_new, l

    m0 = jnp.full((1, tq), NEG, F32)
    _, l = lax.fori_loop(0, nk, body, (m0, jnp.zeros((1, tq), F32)))
    o_ref[...] = (acc_ref[...] / l).T.astype(BF16)


def _attention(qp, kp, vt, nb, seq, tk):
    m = qp.shape[0]
    tq = _pick(seq, 512, 256, 128)
    nq = seq // tq
    nk = seq // tk
    return pl.pallas_call(
        functools.partial(_attn_kernel, tk=tk, nk=nk),
        grid=(nb, HEADS, nq),
        in_specs=[
            pl.BlockSpec((tq, Q_PAD), lambda b, h, i: (b * nq + i, h)),
            pl.BlockSpec((seq, Q_PAD), lambda b, h, i: (b, h)),
            pl.BlockSpec((1, nk, HEAD_DIM, tk), lambda b, h, i: (b, 0, h, 0)),
        ],
        out_specs=pl.BlockSpec((tq, HEAD_DIM), lambda b, h, i: (b * nq + i, h)),
        out_shape=jax.ShapeDtypeStruct((m, HG_WIDTH), BF16),
        scratch_shapes=[pltpu.VMEM((HEAD_DIM, tq), F32)],
        compiler_params=_params("parallel", "parallel", "arbitrary"),
        name="mla_attention",
    )(qp, kp, vt)


def _mm_res_kernel(*refs, n_in):
    res_ref, o_ref = refs[2 * n_in], refs[2 * n_in + 1]
    acc = res_ref[...]
    for a_ref, w_ref in zip(refs[:n_in], refs[n_in:2 * n_in]):
        acc = acc + _dot(a_ref[...], w_ref[...])
    o_ref[...] = acc


def _mm_res(a_list, w_list, res, tm, tn):
    m, n = res.shape
    n_in = len(a_list)
    in_specs = [pl.BlockSpec((tm, a.shape[1]), lambda i, j: (i, 0)) for a in a_list]
    in_specs += [pl.BlockSpec((w.shape[0], tn), lambda i, j: (0, j)) for w in w_list]
    in_specs += [pl.BlockSpec((tm, tn), lambda i, j: (i, j))]
    return pl.pallas_call(
        functools.partial(_mm_res_kernel, n_in=n_in),
        grid=(m // tm, n // tn),
        in_specs=in_specs,
        out_specs=pl.BlockSpec((tm, tn), lambda i, j: (i, j)),
        out_shape=jax.ShapeDtypeStruct((m, n), F32),
        compiler_params=_params("parallel", "arbitrary"),
        name="mm_residual",
    )(*a_list, *w_list, res)


HALO = 16


def _ffn_up_kernel(x_ref, xp_ref, xn_ref, g_ref, wg_ref, wu_ref, cw_ref, cb_ref, o_ref, n_ref,
                   *, tm, tiles_per_seq):
    i = pl.program_id(0)

    @pl.when(pl.program_id(1) == 0)
    def _():
        g = g_ref[...]
        first = (i % tiles_per_seq) == 0
        last = (i % tiles_per_seq) == tiles_per_seq - 1
        n_ref[0:HALO, :] = jnp.where(first, 0.0, _rms(xp_ref[...], g)).astype(BF16)
        n_ref[HALO:HALO + tm, :] = _rms(x_ref[...], g).astype(BF16)
        n_ref[HALO + tm:, :] = jnp.where(last, 0.0, _rms(xn_ref[...], g)).astype(BF16)

    ge = _dot(n_ref[...], wg_ref[...])
    rows = tm + 2 * HALO
    prev = pltpu.roll(ge, 1, 0)[HALO:HALO + tm, :]
    nxt = pltpu.roll(ge, rows - 1, 0)[HALO:HALO + tm, :]
    cw = cw_ref[...]
    gate = prev * cw[0:1, :] + ge[HALO:HALO + tm, :] * cw[1:2, :] + nxt * cw[2:3, :] + cb_ref[...]
    up = _dot(n_ref[HALO:HALO + tm, :], wu_ref[...])
    o_ref[...] = (gate * _sigmoid(gate) * up).astype(BF16)


def _ffn_up(h, g, w_up, conv_w, conv_b, seq):
    m, k = h.shape
    tm = _pick(seq, 1024, 512, 256, 128)
    tf = 512
    nf = D_FF // tf
    tps = seq // tm
    r = tm // HALO
    nblk = m // HALO
    return pl.pallas_call(
        functools.partial(_ffn_up_kernel, tm=tm, tiles_per_seq=tps),
        grid=(m // tm, nf),
        in_specs=[
            pl.BlockSpec((tm, k), lambda i, j: (i, 0)),
            pl.BlockSpec((HALO, k), lambda i, j: (jnp.maximum(i * r - 1, 0), 0)),
            pl.BlockSpec((HALO, k), lambda i, j: (jnp.minimum((i + 1) * r, nblk - 1), 0)),
            pl.BlockSpec((1, k), lambda i, j: (0, 0)),
            pl.BlockSpec((k, tf), lambda i, j: (0, j)),
            pl.BlockSpec((k, tf), lambda i, j: (0, j + nf)),
            pl.BlockSpec((3, tf), lambda i, j: (0, j)),
            pl.BlockSpec((1, tf), lambda i, j: (0, j)),
        ],
        out_specs=pl.BlockSpec((tm, tf), lambda i, j: (i, j)),
        out_shape=jax.ShapeDtypeStruct((m, D_FF), BF16),
        scratch_shapes=[pltpu.VMEM((tm + 2 * HALO, k), BF16)],
        compiler_params=_params("parallel", "arbitrary"),
        name="ffn_up",
    )(h, h, h, g.reshape(1, k), w_up, w_up, conv_w, conv_b.reshape(1, D_FF))


def _ple_kernel(h_ref, p_ref, g_ref, wg_ref, wp_ref, fn_ref, o_ref, *, final, tn):
    x = h_ref[...]
    xn = _rms(x, g_ref[...]).astype(BF16)
    pb = p_ref[...].astype(BF16)
    for c in range(D_MODEL // tn):
        sl = slice(c * tn, (c + 1) * tn)
        gate = _dot(xn, wg_ref[:, sl])
        o_ref[:, sl] = x[:, sl] + _dot(pb, wp_ref[:, sl]) * _sigmoid(gate)
    if final:
        o_ref[...] = _rms(o_ref[...], fn_ref[...])


def _ple(h, p, g, wg, wp, fn, final):
    m, k = h.shape
    tm = _pick(m, 512, 256, 128)
    return pl.pallas_call(
        functools.partial(_ple_kernel, final=final, tn=512),
        grid=(m // tm,),
        in_specs=[
            pl.BlockSpec((tm, k), lambda i: (i, 0)),
            pl.BlockSpec((tm, PLE_DIM), lambda i: (i, 0)),
            pl.BlockSpec((1, k), lambda i: (0, 0)),
            pl.BlockSpec(wg.shape, lambda i: (0, 0)),
            pl.BlockSpec(wp.shape, lambda i: (0, 0)),
            pl.BlockSpec((1, k), lambda i: (0, 0)),
        ],
        out_specs=pl.BlockSpec((tm, k), lambda i: (i, 0)),
        out_shape=jax.ShapeDtypeStruct((m, k), F32),
        compiler_params=_params("parallel"),
        name="ple_gate",
    )(h, p, g.reshape(1, k), wg, wp, fn.reshape(1, k))


def _rot_cols(w):
    half = ROPE_DIM // 2
    return jnp.concatenate([-w[..., half:], w[..., :half]], axis=-1)


def _pad_rope(w):
    return jnp.concatenate([w, jnp.zeros_like(w)], axis=-1)


def _prepare(attn_norm, w_in, hg_lower, hg_norm, w_q_b, w_kv_b, w_out, w_up, w_down,
             w_ple_gate, w_ple_proj):
    w_kr = w_in[:, :, OFF_KR:]
    wq = w_q_b.reshape(DEPTH, LORA, HEADS, HEAD_DIM + ROPE_DIM)
    wq_r = wq[..., HEAD_DIM:]
    wq = jnp.concatenate([wq[..., :HEAD_DIM], _pad_rope(wq_r), _pad_rope(_rot_cols(wq_r))], axis=-1)
    wkv = w_kv_b.reshape(DEPTH, LORA, HEADS, 2 * HEAD_DIM)

    lb = jnp.cumsum(jax.nn.softmax(hg_lower.astype(F32), axis=1), axis=1)
    lb = jnp.maximum(lb - lb[:, :1], 0.0)
    loglb = jnp.maximum(jnp.log(lb), NEG).reshape(2, DEPTH, HEADS, HEAD_DIM)
    log1m = jnp.log1p(-lb).reshape(2, DEPTH, HEADS, HEAD_DIM)
    gain = hg_norm.astype(F32).reshape(DEPTH, HEADS, HEAD_DIM)
    zeros = jnp.zeros_like(gain)
    crow = jnp.stack([loglb[0], log1m[0], loglb[1], log1m[1], gain, zeros, zeros, zeros], axis=2)

    return dict(
        w_in=w_in[:, :, :OFF_KR].astype(BF16),
        w_kr=jnp.concatenate([_pad_rope(w_kr), _pad_rope(_rot_cols(w_kr))], axis=-1).astype(BF16),
        w_q=wq.reshape(DEPTH, LORA, HEADS * 3 * LANES).astype(BF16),
        w_k=wkv[..., :HEAD_DIM].reshape(DEPTH, LORA, HG_WIDTH).astype(BF16),
        w_vt=jnp.swapaxes(wkv[..., HEAD_DIM:].reshape(DEPTH, LORA, HG_WIDTH), 1, 2).astype(BF16),
        w_out=w_out.astype(BF16),
        w_up=w_up.astype(BF16),
        w_down=w_down.astype(BF16),
        w_ple_gate=w_ple_gate.astype(BF16),
        w_ple_proj=w_ple_proj.astype(BF16),
        crow=crow,
    )


def _rope_tables(seq):
    inv = ROPE_THETA ** (-jnp.arange(0, ROPE_DIM, 2, dtype=F32) / ROPE_DIM)
    ang = jnp.arange(seq, dtype=F32)[:, None] * inv[None, :]
    pad = jnp.zeros((seq, LANES - ROPE_DIM), F32)
    cosz = jnp.concatenate([jnp.cos(ang), jnp.cos(ang), pad], axis=-1)
    sinz = jnp.concatenate([jnp.sin(ang), jnp.sin(ang), pad], axis=-1)
    return cosz, sinz


def _trunk(x, p, attn_norm, q_a_norm, kv_a_norm, ffn_norm, conv_w, conv_b, ple_norm, final_norm, prm):
    nb, seq, d = x.shape
    m = nb * seq
    h = x.reshape(m, d)
    p = p.reshape(DEPTH, m, PLE_DIM)
    cosz, sinz = _rope_tables(seq)
    tm = _pick(seq, 1024, 512, 256, 128)
    tk = _pick(seq, 512, 256, 128)
    for l in range(DEPTH):
        z = _norm_mm(h, attn_norm[l], prm["w_in"][l], tm, 512)
        zkr = _norm_mm(h, attn_norm[l], prm["w_kr"][l], tm, 2 * LANES)
        o_hg = _hgrn(z, prm["crow"][l], nb, seq)
        qp = _qproj(z, q_a_norm[l], prm["w_q"][l], cosz, sinz, seq)
        kp, vt = _kvproj(z, zkr, kv_a_norm[l], prm["w_k"][l], prm["w_vt"][l], cosz, sinz, nb, seq, tk)
        o_mla = _attention(qp, kp, vt, nb, seq, tk)
        w_out = prm["w_out"][l]
        h = _mm_res([o_hg, o_mla], [w_out[:HG_WIDTH], w_out[HG_WIDTH:]], h, tm, 512)
        act = _ffn_up(h, ffn_norm[l], prm["w_up"][l], conv_w[l], conv_b[l], seq)
        h = _mm_res([act], [prm["w_down"][l]], h, _pick(seq, 512, 256, 128), 512)
        h = _ple(h, p[l], ple_norm[l], prm["w_ple_gate"][l], prm["w_ple_proj"][l], final_norm,
                 final=(l == DEPTH - 1))
    return h.reshape(nb, seq, d)


def kernel(x_prompt, x_sample, p_prompt, p_sample, attn_norm, w_in, hg_lower, hg_norm, q_a_norm,
           w_q_b, kv_a_norm, w_kv_b, w_out, ffn_norm, w_up, conv_w, conv_b, w_down, ple_norm,
           w_ple_gate, w_ple_proj, final_norm):
    assert x_prompt.shape[1:] == x_sample.shape[1:], "both groups must share the sequence length"
    nbp = x_prompt.shape[0]
    prm = _prepare(attn_norm, w_in, hg_lower, hg_norm, w_q_b, w_kv_b, w_out, w_up, w_down,
                   w_ple_gate, w_ple_proj)
    x = jnp.concatenate([x_prompt, x_sample], axis=0)
    p = jnp.concatenate([p_prompt, p_sample], axis=1)
    y = _trunk(x, p, attn_norm, q_a_norm, kv_a_norm, ffn_norm, conv_w, conv_b, ple_norm, final_norm, prm)
    return (y[:nbp], y[nbp:])
```

```python
import functools
import math

import jax
import jax.numpy as jnp
import numpy as np
from jax import lax
from jax.experimental import pallas as pl
from jax.experimental.pallas import tpu as pltpu

F32 = jnp.float32
BF16 = jnp.bfloat16

D_MODEL = 2048
DEPTH = 4
HEADS = 8
HEAD_DIM = 128
ROPE_DIM = 64
LORA = 512
HG_WIDTH = HEADS * HEAD_DIM
OFF_Q, OFF_FF, OFF_FB, OFF_I, OFF_G, OFF_QA, OFF_KVA, OFF_KR = (
    0, 1024, 2048, 3072, 4096, 5120, 5632, 6144)
D_FF = 5632
PLE_DIM = 256
EPS = 1e-6
ROPE_THETA = 10000.0
NEG = -1e30

LANES = 128
SUBLANES = 8
VMEM_LIMIT = 56 * 2**20

HG_CHUNK = 64
HG_ROWS = SUBLANES
HG_STEP = 512
HG_UNROLL = 2
Q_PAD = 2 * LANES
LOG2E = math.log2(math.e)
QK_SCALE = (HEAD_DIM + ROPE_DIM) ** -0.5 * LOG2E
ATT_GROUP = 2 * LANES


def _params(*sem):
    return pltpu.CompilerParams(dimension_semantics=sem, vmem_limit_bytes=VMEM_LIMIT)


def _pick(n, *cands):
    for c in cands:
        if n % c == 0:
            return c
    raise ValueError(f"no tile for {n} in {cands}")


def _rms(x, g):
    ms = jnp.mean(x * x, axis=-1, keepdims=True)
    return x * lax.rsqrt(ms + EPS) * g


def _sigmoid(x):
    return 1.0 / (1.0 + jnp.exp(-x))


def _dot(a, b):
    return jnp.dot(a, b, preferred_element_type=F32)


def _dot_nt(a, b):
    return lax.dot_general(a, b, (((1,), (1,)), ((), ())), preferred_element_type=F32)


def _dot_tn(a, b):
    return lax.dot_general(a, b, (((0,), (0,)), ((), ())), preferred_element_type=F32)


def _norm_mm_kernel(x_ref, g_ref, w_ref, o_ref, xn_ref):
    @pl.when(pl.program_id(1) == 0)
    def _():
        xn_ref[...] = _rms(x_ref[...], g_ref[...]).astype(BF16)

    o_ref[...] = _dot(xn_ref[...], w_ref[...]).astype(o_ref.dtype)


def _norm_mm(x, g, w, tm, tn):
    m, k = x.shape
    n = w.shape[1]
    return pl.pallas_call(
        _norm_mm_kernel,
        grid=(m // tm, n // tn),
        in_specs=[
            pl.BlockSpec((tm, k), lambda i, j: (i, 0)),
            pl.BlockSpec((1, k), lambda i, j: (0, 0)),
            pl.BlockSpec((k, tn), lambda i, j: (0, j)),
        ],
        out_specs=pl.BlockSpec((tm, tn), lambda i, j: (i, j)),
        out_shape=jax.ShapeDtypeStruct((m, n), F32),
        scratch_shapes=[pltpu.VMEM((tm, k), BF16)],
        compiler_params=_params("parallel", "arbitrary"),
        name="norm_mm",
    )(x, g.reshape(1, k), w)


def _hgrn_gates(z, loglb, log1m):
    l1p = jnp.log(1.0 + jnp.exp(-jnp.abs(z)))
    c = log1m + (jnp.minimum(z, 0.0) - l1p)
    logf = jnp.maximum(loglb, c) + jnp.log(1.0 + jnp.exp(-jnp.abs(loglb - c)))
    logk = log1m + (jnp.minimum(-z, 0.0) - l1p)
    return logf, logk


def _hgrn_chunk(q, zf, v, cmat, lvl, emat, loglb, log1m, st_ref, reverse):
    c = q.shape[0]
    nv = c // HG_ROWS
    logf, logk = _hgrn_gates(zf, loglb, log1m)
    logf = logf * LOG2E
    logk = logk * LOG2E

    hi = logf.astype(BF16)
    r1 = logf - hi.astype(F32)
    mid = r1.astype(BF16)
    lo = (r1 - mid.astype(F32)).astype(BF16)
    yield
    bb = _dot(cmat, jnp.concatenate([hi, mid, lo], axis=1))
    b = bb[:, :LANES] + bb[:, LANES:2 * LANES] + bb[:, 2 * LANES:]

    def rows(x, p):
        i = nv - 1 - p if reverse else p
        return x[i * HG_ROWS:(i + 1) * HG_ROWS, :]

    def assemble(groups):
        return jnp.concatenate(groups[::-1] if reverse else groups, axis=0)

    edge = 0 if reverse else HG_ROWS - 1
    b_p = [rows(b, p) for p in range(nv)]
    q_p = [rows(q, p) for p in range(nv)]
    lk_p = [rows(logk, p) for p in range(nv)]
    bnd = [jnp.broadcast_to(x[edge:edge + 1, :], (HG_ROWS, LANES)) for x in b_p]
    zero = jnp.zeros((HG_ROWS, LANES), F32)

    a = jnp.zeros((c, c), F32)
    level, mv = 1, nv // 2
    while mv >= 1:
        qm = [zero] * nv
        km = [zero] * nv
        for g in range(nv // (2 * mv)):
            ref = bnd[2 * mv * g + mv - 1]
            for p in range(2 * mv * g, 2 * mv * g + mv):
                km[p] = jnp.exp2(lk_p[p] + (ref - b_p[p]))
            for p in range(2 * mv * g + mv, 2 * mv * (g + 1)):
                qm[p] = q_p[p] * jnp.exp2(b_p[p] - ref)
        yield
        al = _dot_nt(assemble(qm).astype(BF16), assemble(km).astype(BF16))
        a = jnp.where(lvl == float(level), al, a)
        level += 1
        mv //= 2

    rib = lax.broadcasted_iota(jnp.int32, (HG_ROWS, LANES), 0)
    slabs = []
    for s in range(HG_ROWS):
        valid = (rib <= s) if reverse else (rib >= s)
        col = []
        for p in range(nv):
            cs = jnp.broadcast_to((b_p[p] - lk_p[p])[s:s + 1, :], (HG_ROWS, LANES))
            col.append(q_p[p] * jnp.exp2(jnp.where(valid, b_p[p] - cs, NEG)))
        slabs.append(assemble(col).astype(BF16))
        yield
    adiag = _dot(jnp.concatenate(slabs, axis=1), emat)
    a = jnp.where(lvl == float(level), adiag, a)

    tot = bnd[nv - 1]
    qbar = assemble([q_p[p] * jnp.exp2(b_p[p]) for p in range(nv)]).astype(BF16)
    kbar = assemble([jnp.exp2(lk_p[p] + (tot - b_p[p])) for p in range(nv)]).astype(BF16)
    vb = v.astype(BF16)
    yield
    st = st_ref[...]
    o = _dot(a.astype(BF16), vb) + _dot_nt(qbar, st.astype(BF16))
    st_ref[...] = jnp.exp2(tot[0:1, :]) * st + _dot_tn(vb, kbar)
    return o


def _lockstep(gens):
    results = [None] * len(gens)
    live = list(range(len(gens)))
    while live:
        for idx in list(live):
            try:
                next(gens[idx])
            except StopIteration as done:
                results[idx] = done.value
                live.remove(idx)
    return results


def _hgrn_kernel(qf_ref, zf_ref, vf_ref, gf_ref, qb_ref, zb_ref, vb_ref, gb_ref,
                 crow_ref, cmat_ref, lvl_ref, emat_ref, o_ref,
                 part_ref, stf_ref, stb_ref, *, step, chunk, nsteps):
    c = pl.program_id(2)

    @pl.when(c == 0)
    def _():
        stf_ref[...] = jnp.zeros_like(stf_ref)
        stb_ref[...] = jnp.zeros_like(stb_ref)
        part_ref[...] = jnp.zeros_like(part_ref)

    crow = crow_ref[0]
    gain = crow[4:5, :]
    emat = emat_ref[...]
    nch = step // chunk

    def emit(o, row, g):
        t = o + part_ref[pl.ds(row, chunk), :]
        part_ref[pl.ds(row, chunk), :] = t
        o_ref[pl.ds(row, chunk), :] = (_rms(t, gain) * (g * _sigmoid(g))).astype(BF16)

    def fwd(j):
        rf = pl.multiple_of(j * chunk, chunk)
        sl = pl.ds(rf, chunk)
        gen = _hgrn_chunk(qf_ref[sl, :], zf_ref[sl, :], vf_ref[sl, :], cmat_ref[0], lvl_ref[0], emat,
                          crow[0:1, :], crow[1:2, :], stf_ref, False)
        return gen, pl.multiple_of(c * step + rf, chunk), gf_ref, sl

    def bwd(j):
        rb = pl.multiple_of((nch - 1 - j) * chunk, chunk)
        sl = pl.ds(rb, chunk)
        gen = _hgrn_chunk(qb_ref[sl, :], zb_ref[sl, :], vb_ref[sl, :], cmat_ref[1], lvl_ref[1], emat,
                          crow[2:3, :], crow[3:4, :], stb_ref, True)
        return gen, pl.multiple_of((nsteps - 1 - c) * step + rb, chunk), gb_ref, sl

    def body(i, carry):
        chains = [f(i * HG_UNROLL + u) for u in range(HG_UNROLL) for f in (fwd, bwd)]
        outs = _lockstep([ch[0] for ch in chains])
        for o, (_, row, g_ref, sl) in zip(outs, chains):
            emit(o, row, g_ref[sl, :])
        return carry

    lax.fori_loop(0, nch // HG_UNROLL, body, 0)


def _hgrn_consts(chunk):
    t = np.arange(chunk)[:, None]
    s = np.arange(chunk)[None, :]
    cm = np.stack([(s <= t), (s >= t)]).astype(np.float32)
    nlev = int(round(math.log2(chunk // HG_ROWS)))
    lv = np.zeros((chunk, chunk), np.float32)
    for level in range(1, nlev + 1):
        m = chunk >> level
        hit = (t // (2 * m) == s // (2 * m)) & (t // m != s // m) & (s < t)
        lv[hit] = level
    lv[(t // HG_ROWS) == (s // HG_ROWS)] = nlev + 1
    off = (lv >= 1) & (lv <= nlev)
    lvb = np.where(off.T, lv.T, 0.0)
    lvb[(t // HG_ROWS) == (s // HG_ROWS)] = nlev + 1
    lvl = np.stack([lv, lvb]).astype(np.float32)
    em = (np.arange(HG_ROWS * LANES)[:, None] // LANES == (np.arange(chunk)[None, :] % HG_ROWS))
    return jnp.asarray(cm, BF16), jnp.asarray(lvl, F32), jnp.asarray(em.astype(np.float32), BF16)


def _hgrn(z, crow, nb, seq):
    m = z.shape[0]
    step = _pick(seq // 2, HG_STEP, 256, 128, 64)
    chunk = HG_CHUNK
    n = seq // step
    half = n // 2
    cmat, lvl, emat = _hgrn_consts(chunk)

    def zspec(col, fwd, gate=False):
        if fwd:
            blk = (lambda c: jnp.maximum(c, half)) if gate else (lambda c: c)
        else:
            blk = (lambda c: jnp.minimum(n - 1 - c, half - 1)) if gate else (lambda c: n - 1 - c)
        return pl.BlockSpec((step, LANES), lambda b, h, c: (b * n + blk(c), col // LANES + h))

    const3 = lambda shape: pl.BlockSpec(shape, lambda b, h, c: (0, 0, 0))
    return pl.pallas_call(
        functools.partial(_hgrn_kernel, step=step, chunk=chunk, nsteps=n),
        grid=(nb, HEADS, n),
        in_specs=[
            zspec(OFF_Q, True), zspec(OFF_FF, True), zspec(OFF_I, True), zspec(OFF_G, True, True),
            zspec(OFF_Q, False), zspec(OFF_FB, False), zspec(OFF_I, False), zspec(OFF_G, False, True),
            pl.BlockSpec((1, SUBLANES, LANES), lambda b, h, c: (h, 0, 0)),
            const3(cmat.shape), const3(lvl.shape),
            pl.BlockSpec(emat.shape, lambda b, h, c: (0, 0)),
        ],
        out_specs=pl.BlockSpec((seq, LANES), lambda b, h, c: (b, h)),
        out_shape=jax.ShapeDtypeStruct((m, HG_WIDTH), BF16),
        scratch_shapes=[
            pltpu.VMEM((seq, LANES), F32),
            pltpu.VMEM((HEAD_DIM, HEAD_DIM), F32),
            pltpu.VMEM((HEAD_DIM, HEAD_DIM), F32),
        ],
        compiler_params=_params("parallel", "parallel", "arbitrary"),
        name="hgrn2",
    )(z, z, z, z, z, z, z, z, crow, cmat, lvl, emat)


def _qproj_kernel(z_ref, g_ref, w_ref, cos_ref, sin_ref, o_ref):
    xn = _rms(z_ref[...], g_ref[...]).astype(BF16)
    cz = cos_ref[...]
    sz = sin_ref[...]
    for h in range(HEADS):
        acc = _dot(xn, w_ref[:, h * 3 * LANES:(h + 1) * 3 * LANES])
        o_ref[:, h * Q_PAD:h * Q_PAD + LANES] = (acc[:, :LANES] * QK_SCALE).astype(BF16)
        rope = acc[:, LANES:2 * LANES] * cz + acc[:, 2 * LANES:] * sz
        o_ref[:, h * Q_PAD + LANES:(h + 1) * Q_PAD] = (rope * QK_SCALE).astype(BF16)


def _qproj(z, g, w, cosz, sinz, seq):
    m = z.shape[0]
    tm = _pick(seq, 512, 256, 128)
    ns = seq // tm
    return pl.pallas_call(
        _qproj_kernel,
        grid=(m // tm,),
        in_specs=[
            pl.BlockSpec((tm, LORA), lambda i: (i, OFF_QA // LORA)),
            pl.BlockSpec((1, LORA), lambda i: (0, 0)),
            pl.BlockSpec(w.shape, lambda i: (0, 0)),
            pl.BlockSpec((tm, LANES), lambda i: (i % ns, 0)),
            pl.BlockSpec((tm, LANES), lambda i: (i % ns, 0)),
        ],
        out_specs=pl.BlockSpec((tm, HEADS * Q_PAD), lambda i: (i, 0)),
        out_shape=jax.ShapeDtypeStruct((m, HEADS * Q_PAD), BF16),
        compiler_params=_params("parallel"),
        name="mla_q_proj",
    )(z, g.reshape(1, LORA), w, cosz, sinz)


def _kvproj_kernel(z_ref, kr_ref, g_ref, wk_ref, wvt_ref, cos_ref, sin_ref, k_ref, vt_ref):
    xn = _rms(z_ref[...], g_ref[...]).astype(BF16)
    kr = kr_ref[...]
    krope = (kr[:, :LANES] * cos_ref[...] + kr[:, LANES:] * sin_ref[...]).astype(BF16)
    kn = _dot(xn, wk_ref[...])
    for h in range(HEADS):
        k_ref[:, h * Q_PAD:h * Q_PAD + LANES] = kn[:, h * LANES:(h + 1) * LANES].astype(BF16)
        k_ref[:, h * Q_PAD + LANES:(h + 1) * Q_PAD] = krope
    vt_ref[0, 0] = _dot_nt(wvt_ref[...], xn).astype(BF16)


def _kvproj(z, zkr, g, wk, wvt, cosz, sinz, nb, seq, tk):
    m = z.shape[0]
    ns = seq // tk
    return pl.pallas_call(
        _kvproj_kernel,
        grid=(m // tk,),
        in_specs=[
            pl.BlockSpec((tk, LORA), lambda i: (i, OFF_KVA // LORA)),
            pl.BlockSpec((tk, 2 * LANES), lambda i: (i, 0)),
            pl.BlockSpec((1, LORA), lambda i: (0, 0)),
            pl.BlockSpec(wk.shape, lambda i: (0, 0)),
            pl.BlockSpec(wvt.shape, lambda i: (0, 0)),
            pl.BlockSpec((tk, LANES), lambda i: (i % ns, 0)),
            pl.BlockSpec((tk, LANES), lambda i: (i % ns, 0)),
        ],
        out_specs=[
            pl.BlockSpec((tk, HEADS * Q_PAD), lambda i: (i, 0)),
            pl.BlockSpec((1, 1, HG_WIDTH, tk), lambda i: (i // ns, i % ns, 0, 0)),
        ],
        out_shape=[
            jax.ShapeDtypeStruct((m, HEADS * Q_PAD), BF16),
            jax.ShapeDtypeStruct((nb, ns, HG_WIDTH, tk), BF16),
        ],
        compiler_params=_params("parallel"),
        name="mla_kv_proj",
    )(z, zkr, g.reshape(1, LORA), wk, wvt, cosz, sinz)


def _col_reduce(x, op, slab=64):
    r, c = x.shape
    if r > slab and r % slab == 0:
        x = op(x.reshape(r // slab, slab, c), axis=0)
    return op(x, axis=0, keepdims=True)


def _attn_kernel(q_ref, k_ref, vt_ref, o_ref, acc_ref, st_ref, *, tk, nk):
    tq = q_ref.shape[0]
    gw = st_ref.shape[-1]
    ng = tq // gw
    acc_ref[...] = jnp.zeros_like(acc_ref)

    def scores(j, slot):
        kb = k_ref[pl.ds(pl.multiple_of(j * tk, tk), tk), :]
        for g in range(ng):
            st_ref[slot, g] = _dot_nt(kb, q_ref[g * gw:(g + 1) * gw, :])

    def consume(j, slot, carry):
        vt = vt_ref[0, j]
        out = []
        for g in range(ng):
            m, l = carry[g]
            cols = slice(g * gw, (g + 1) * gw)
            st = st_ref[slot, g]
            m_new = jnp.maximum(m, _col_reduce(st, jnp.max))
            alpha = jnp.exp2(m - m_new)
            p = jnp.exp2(st - m_new)
            l = alpha * l + _col_reduce(p, jnp.sum)
            acc_ref[:, cols] = alpha * acc_ref[:, cols] + _dot(vt, p.astype(BF16))
            out.append((m_new, l))
        return tuple(out)

    def body(i, carry):
        j = 2 * i
        scores(j + 1, 1)
        carry = consume(j, 0, carry)
        scores(jnp.minimum(j + 2, nk - 1), 0)
        return consume(j + 1, 1, carry)

    scores(0, 0)
    init = tuple((jnp.full((1, gw), NEG, F32), jnp.zeros((1, gw), F32)) for _ in range(ng))
    fin = lax.fori_loop(0, nk // 2, body, init)
    for g in range(ng):
        cols = slice(g * gw, (g + 1) * gw)
        o_ref[cols, :] = (acc_ref[:, cols] / fin[g][1]).T.astype(BF16)


def _attention(qp, kp, vt, nb, seq, tk):
    m = qp.shape[0]
    tq = _pick(seq, 1024, 512, 256, 128)
    nq = seq // tq
    nk = seq // tk
    assert nk % 2 == 0, "the key loop handles two blocks per trip"
    gw = min(ATT_GROUP, tq)
    return pl.pallas_call(
        functools.partial(_attn_kernel, tk=tk, nk=nk),
        grid=(nb, HEADS, nq),
        in_specs=[
            pl.BlockSpec((tq, Q_PAD), lambda b, h, i: (b * nq + i, h)),
            pl.BlockSpec((seq, Q_PAD), lambda b, h, i: (b, h)),
            pl.BlockSpec((1, nk, HEAD_DIM, tk), lambda b, h, i: (b, 0, h, 0)),
        ],
        out_specs=pl.BlockSpec((tq, HEAD_DIM), lambda b, h, i: (b * nq + i, h)),
        out_shape=jax.ShapeDtypeStruct((m, HG_WIDTH), BF16),
        scratch_shapes=[pltpu.VMEM((HEAD_DIM, tq), F32), pltpu.VMEM((2, tq // gw, tk, gw), F32)],
        compiler_params=_params("parallel", "parallel", "arbitrary"),
        name="mla_attention",
    )(qp, kp, vt)


def _mm_res_kernel(*refs, n_in):
    res_ref, o_ref = refs[2 * n_in], refs[2 * n_in + 1]
    acc = res_ref[...]
    for a_ref, w_ref in zip(refs[:n_in], refs[n_in:2 * n_in]):
        acc = acc + _dot(a_ref[...], w_ref[...])
    o_ref[...] = acc


def _mm_res(a_list, w, res, tm, tn):
    m, n = res.shape
    n_in = len(a_list)
    kb = a_list[0].shape[1]
    assert all(a.shape[1] == kb for a in a_list) and w.shape[0] == n_in * kb
    w_list = [w] * n_in
    in_specs = [pl.BlockSpec((tm, kb), lambda i, j: (i, 0)) for _ in a_list]
    in_specs += [pl.BlockSpec((kb, tn), functools.partial(lambda i, j, r: (r, j), r=r))
                 for r in range(n_in)]
    in_specs += [pl.BlockSpec((tm, tn), lambda i, j: (i, j))]
    return pl.pallas_call(
        functools.partial(_mm_res_kernel, n_in=n_in),
        grid=(m // tm, n // tn),
        in_specs=in_specs,
        out_specs=pl.BlockSpec((tm, tn), lambda i, j: (i, j)),
        out_shape=jax.ShapeDtypeStruct((m, n), F32),
        compiler_params=_params("parallel", "arbitrary"),
        name="mm_residual",
    )(*a_list, *w_list, res)


HALO = 16


def _ffn_up_kernel(x_ref, xp_ref, xn_ref, g_ref, wg_ref, wu_ref, cw_ref, cb_ref, o_ref, n_ref,
                   *, tm, tiles_per_seq):
    i = pl.program_id(0)

    @pl.when(pl.program_id(1) == 0)
    def _():
        g = g_ref[...]
        first = (i % tiles_per_seq) == 0
        last = (i % tiles_per_seq) == tiles_per_seq - 1
        n_ref[0:HALO, :] = jnp.where(first, 0.0, _rms(xp_ref[...], g)).astype(BF16)
        n_ref[HALO:HALO + tm, :] = _rms(x_ref[...], g).astype(BF16)
        n_ref[HALO + tm:, :] = jnp.where(last, 0.0, _rms(xn_ref[...], g)).astype(BF16)

    ge = _dot(n_ref[...], wg_ref[...])
    rows = tm + 2 * HALO
    prev = pltpu.roll(ge, 1, 0)[HALO:HALO + tm, :]
    nxt = pltpu.roll(ge, rows - 1, 0)[HALO:HALO + tm, :]
    cw = cw_ref[...]
    gate = prev * cw[0:1, :] + ge[HALO:HALO + tm, :] * cw[1:2, :] + nxt * cw[2:3, :] + cb_ref[...]
    up = _dot(n_ref[HALO:HALO + tm, :], wu_ref[...])
    o_ref[...] = (gate * _sigmoid(gate) * up).astype(BF16)


def _ffn_up(h, g, w_up, conv_w, conv_b, seq):
    m, k = h.shape
    tm = _pick(seq, 1024, 512, 256, 128)
    tf = 512
    nf = D_FF // tf
    tps = seq // tm
    r = tm // HALO
    nblk = m // HALO
    return pl.pallas_call(
        functools.partial(_ffn_up_kernel, tm=tm, tiles_per_seq=tps),
        grid=(m // tm, nf),
        in_specs=[
            pl.BlockSpec((tm, k), lambda i, j: (i, 0)),
            pl.BlockSpec((HALO, k), lambda i, j: (jnp.maximum(i * r - 1, 0), 0)),
            pl.BlockSpec((HALO, k), lambda i, j: (jnp.minimum((i + 1) * r, nblk - 1), 0)),
            pl.BlockSpec((1, k), lambda i, j: (0, 0)),
            pl.BlockSpec((k, tf), lambda i, j: (0, j)),
            pl.BlockSpec((k, tf), lambda i, j: (0, j + nf)),
            pl.BlockSpec((3, tf), lambda i, j: (0, j)),
            pl.BlockSpec((1, tf), lambda i, j: (0, j)),
        ],
        out_specs=pl.BlockSpec((tm, tf), lambda i, j: (i, j)),
        out_shape=jax.ShapeDtypeStruct((m, D_FF), BF16),
        scratch_shapes=[pltpu.VMEM((tm + 2 * HALO, k), BF16)],
        compiler_params=_params("parallel", "arbitrary"),
        name="ffn_up",
    )(h, h, h, g.reshape(1, k), w_up, w_up, conv_w, conv_b.reshape(1, D_FF))


def _ple_kernel(h_ref, p_ref, g_ref, wg_ref, wp_ref, fn_ref, o_ref, *, final, tn):
    x = h_ref[...]
    xn = _rms(x, g_ref[...]).astype(BF16)
    pb = p_ref[...].astype(BF16)
    for c in range(D_MODEL // tn):
        sl = slice(c * tn, (c + 1) * tn)
        gate = _dot(xn, wg_ref[:, sl])
        o_ref[:, sl] = x[:, sl] + _dot(pb, wp_ref[:, sl]) * _sigmoid(gate)
    if final:
        o_ref[...] = _rms(o_ref[...], fn_ref[...])


def _ple(h, p, g, wg, wp, fn, final):
    m, k = h.shape
    tm = _pick(m, 512, 256, 128)
    return pl.pallas_call(
        functools.partial(_ple_kernel, final=final, tn=512),
        grid=(m // tm,),
        in_specs=[
            pl.BlockSpec((tm, k), lambda i: (i, 0)),
            pl.BlockSpec((tm, PLE_DIM), lambda i: (i, 0)),
            pl.BlockSpec((1, k), lambda i: (0, 0)),
            pl.BlockSpec(wg.shape, lambda i: (0, 0)),
            pl.BlockSpec(wp.shape, lambda i: (0, 0)),
            pl.BlockSpec((1, k), lambda i: (0, 0)),
        ],
        out_specs=pl.BlockSpec((tm, k), lambda i: (i, 0)),
        out_shape=jax.ShapeDtypeStruct((m, k), F32),
        compiler_params=_params("parallel"),
        name="ple_gate",
    )(h, p, g.reshape(1, k), wg, wp, fn.reshape(1, k))


def _rot_cols(w):
    half = ROPE_DIM // 2
    return jnp.concatenate([-w[..., half:], w[..., :half]], axis=-1)


def _pad_rope(w):
    return jnp.concatenate([w, jnp.zeros_like(w)], axis=-1)


def _prepare(attn_norm, w_in, hg_lower, hg_norm, w_q_b, w_kv_b, w_out, w_up, w_down,
             w_ple_gate, w_ple_proj):
    w_kr = w_in[:, :, OFF_KR:]
    wq = w_q_b.reshape(DEPTH, LORA, HEADS, HEAD_DIM + ROPE_DIM)
    wq_r = wq[..., HEAD_DIM:]
    wq = jnp.concatenate([wq[..., :HEAD_DIM], _pad_rope(wq_r), _pad_rope(_rot_cols(wq_r))], axis=-1)
    wkv = w_kv_b.reshape(DEPTH, LORA, HEADS, 2 * HEAD_DIM)

    lb = jnp.cumsum(jax.nn.softmax(hg_lower.astype(F32), axis=1), axis=1)
    lb = jnp.maximum(lb - lb[:, :1], 0.0)
    loglb = jnp.maximum(jnp.log(lb), NEG).reshape(2, DEPTH, HEADS, HEAD_DIM)
    log1m = jnp.log1p(-lb).reshape(2, DEPTH, HEADS, HEAD_DIM)
    gain = hg_norm.astype(F32).reshape(DEPTH, HEADS, HEAD_DIM)
    zeros = jnp.zeros_like(gain)
    crow = jnp.stack([loglb[0], log1m[0], loglb[1], log1m[1], gain, zeros, zeros, zeros], axis=2)

    return dict(
        w_in=w_in[:, :, :OFF_KR].astype(BF16),
        w_kr=jnp.concatenate([_pad_rope(w_kr), _pad_rope(_rot_cols(w_kr))], axis=-1).astype(BF16),
        w_q=wq.reshape(DEPTH, LORA, HEADS * 3 * LANES).astype(BF16),
        w_k=wkv[..., :HEAD_DIM].reshape(DEPTH, LORA, HG_WIDTH).astype(BF16),
        w_vt=jnp.swapaxes(wkv[..., HEAD_DIM:].reshape(DEPTH, LORA, HG_WIDTH), 1, 2).astype(BF16),
        w_out=w_out.astype(BF16),
        w_up=w_up.astype(BF16),
        w_down=w_down.astype(BF16),
        w_ple_gate=w_ple_gate.astype(BF16),
        w_ple_proj=w_ple_proj.astype(BF16),
        crow=crow,
    )


def _rope_tables(seq):
    inv = ROPE_THETA ** (-jnp.arange(0, ROPE_DIM, 2, dtype=F32) / ROPE_DIM)
    ang = jnp.arange(seq, dtype=F32)[:, None] * inv[None, :]
    pad = jnp.zeros((seq, LANES - ROPE_DIM), F32)
    cosz = jnp.concatenate([jnp.cos(ang), jnp.cos(ang), pad], axis=-1)
    sinz = jnp.concatenate([jnp.sin(ang), jnp.sin(ang), pad], axis=-1)
    return cosz, sinz


def _trunk(x, p, attn_norm, q_a_norm, kv_a_norm, ffn_norm, conv_w, conv_b, ple_norm, final_norm, prm):
    nb, seq, d = x.shape
    m = nb * seq
    h = x.reshape(m, d)
    p = p.reshape(DEPTH, m, PLE_DIM)
    cosz, sinz = _rope_tables(seq)
    tm = _pick(seq, 1024, 512, 256, 128)
    tk = _pick(seq, 512, 256, 128)
    for l in range(DEPTH):
        z = _norm_mm(h, attn_norm[l], prm["w_in"][l], tm, 512)
        zkr = _norm_mm(h, attn_norm[l], prm["w_kr"][l], tm, 2 * LANES)
        o_hg = _hgrn(z, prm["crow"][l], nb, seq)
        qp = _qproj(z, q_a_norm[l], prm["w_q"][l], cosz, sinz, seq)
        kp, vt = _kvproj(z, zkr, kv_a_norm[l], prm["w_k"][l], prm["w_vt"][l], cosz, sinz, nb, seq, tk)
        o_mla = _attention(qp, kp, vt, nb, seq, tk)
        h = _mm_res([o_hg, o_mla], prm["w_out"][l], h, tm, 512)
        act = _ffn_up(h, ffn_norm[l], prm["w_up"][l], conv_w[l], conv_b[l], seq)
        h = _mm_res([act], prm["w_down"][l], h, _pick(seq, 512, 256, 128), 512)
        h = _ple(h, p[l], ple_norm[l], prm["w_ple_gate"][l], prm["w_ple_proj"][l], final_norm,
                 final=(l == DEPTH - 1))
    return h.reshape(nb, seq, d)


def kernel(x_prompt, x_sample, p_prompt, p_sample, attn_norm, w_in, hg_lower, hg_norm, q_a_norm,
           w_q_b, kv_a_norm, w_kv_b, w_out, ffn_norm, w_up, conv_w, conv_b, w_down, ple_norm,
           w_ple_gate, w_ple_proj, final_norm):
    assert x_prompt.shape[1:] == x_sample.shape[1:], "both groups must share the sequence length"
    nbp = x_prompt.shape[0]
    prm = _prepare(attn_norm, w_in, hg_lower, hg_norm, w_q_b, w_kv_b, w_out, w_up, w_down,
                   w_ple_gate, w_ple_proj)
    x = jnp.concatenate([x_prompt, x_sample], axis=0)
    p = jnp.concatenate([p_prompt, p_sample], axis=1)
    y = _trunk(x, p, attn_norm, q_a_norm, kv_a_norm, ffn_norm, conv_w, conv_b, ple_norm, final_norm, prm)
    return (y[:nbp], y[nbp:])
```

```python
import functools
import math

import jax
import jax.numpy as jnp
import numpy as np
from jax import lax
from jax.experimental import pallas as pl
from jax.experimental.pallas import tpu as pltpu

F32 = jnp.float32
BF16 = jnp.bfloat16

D_MODEL = 2048
DEPTH = 4
HEADS = 8
HEAD_DIM = 128
ROPE_DIM = 64
LORA = 512
HG_WIDTH = HEADS * HEAD_DIM
OFF_Q, OFF_FF, OFF_FB, OFF_I, OFF_G, OFF_QA, OFF_KVA, OFF_KR = (
    0, 1024, 2048, 3072, 4096, 5120, 5632, 6144)
D_FF = 5632
PLE_DIM = 256
EPS = 1e-6
ROPE_THETA = 10000.0
NEG = -1e30

LANES = 128
SUBLANES = 8
VMEM_LIMIT = 56 * 2**20

HG_CHUNK = 64
HG_ROWS = SUBLANES
HG_STEP = 512
HG_UNROLL = 4
Q_PAD = 2 * LANES
LOG2E = math.log2(math.e)
QK_SCALE = (HEAD_DIM + ROPE_DIM) ** -0.5 * LOG2E
ATT_GROUP = 2 * LANES
ATT_BLOCKS = 4
VT_ROWS = HEAD_DIM + 16


def _params(*sem):
    return pltpu.CompilerParams(dimension_semantics=sem, vmem_limit_bytes=VMEM_LIMIT)


def _pick(n, *cands):
    for c in cands:
        if n % c == 0:
            return c
    raise ValueError(f"no tile for {n} in {cands}")


def _rms(x, g):
    ms = jnp.mean(x * x, axis=-1, keepdims=True)
    return x * lax.rsqrt(ms + EPS) * g


def _sigmoid(x):
    return 1.0 / (1.0 + jnp.exp(-x))


def _dot(a, b):
    return jnp.dot(a, b, preferred_element_type=F32)


def _dot_nt(a, b):
    return lax.dot_general(a, b, (((1,), (1,)), ((), ())), preferred_element_type=F32)


def _dot_tn(a, b):
    return lax.dot_general(a, b, (((0,), (0,)), ((), ())), preferred_element_type=F32)


def _norm_mm_kernel(x_ref, g_ref, w_ref, o_ref, xn_ref):
    @pl.when(pl.program_id(1) == 0)
    def _():
        xn_ref[...] = _rms(x_ref[...], g_ref[...]).astype(BF16)

    o_ref[...] = _dot(xn_ref[...], w_ref[...]).astype(o_ref.dtype)


def _norm_mm(x, g, w, tm, tn):
    m, k = x.shape
    n = w.shape[1]
    return pl.pallas_call(
        _norm_mm_kernel,
        grid=(m // tm, n // tn),
        in_specs=[
            pl.BlockSpec((tm, k), lambda i, j: (i, 0)),
            pl.BlockSpec((1, k), lambda i, j: (0, 0)),
            pl.BlockSpec((k, tn), lambda i, j: (0, j)),
        ],
        out_specs=pl.BlockSpec((tm, tn), lambda i, j: (i, j)),
        out_shape=jax.ShapeDtypeStruct((m, n), F32),
        scratch_shapes=[pltpu.VMEM((tm, k), BF16)],
        compiler_params=_params("parallel", "arbitrary"),
        name="norm_mm",
    )(x, g.reshape(1, k), w)


def _hgrn_gates(z, loglb, log1m):
    z2 = z * LOG2E
    l1p = jnp.log2(1.0 + jnp.exp2(-jnp.abs(z2)))
    c = log1m + (jnp.minimum(z2, 0.0) - l1p)
    logf = jnp.maximum(loglb, c) + jnp.log2(1.0 + jnp.exp2(-jnp.abs(loglb - c)))
    return logf, c - z2


def _hgrn_chunk(q, zf, v, cmat, lvl, emat, loglb, log1m, st_ref, reverse):
    c = q.shape[0]
    nv = c // HG_ROWS
    logf, logk = _hgrn_gates(zf, loglb, log1m)

    hi = logf.astype(BF16)
    lo = (logf - hi.astype(F32)).astype(BF16)
    yield
    bb = _dot(cmat, jnp.concatenate([hi, lo], axis=1))
    b = bb[:, :LANES] + bb[:, LANES:]

    def rows(x, p):
        i = nv - 1 - p if reverse else p
        return x[i * HG_ROWS:(i + 1) * HG_ROWS, :]

    def assemble(groups):
        return jnp.concatenate(groups[::-1] if reverse else groups, axis=0)

    edge = 0 if reverse else HG_ROWS - 1
    b_p = [rows(b, p) for p in range(nv)]
    q_p = [rows(q, p) for p in range(nv)]
    lk_p = [rows(logk, p) for p in range(nv)]
    bnd = [jnp.broadcast_to(x[edge:edge + 1, :], (HG_ROWS, LANES)) for x in b_p]
    zero = jnp.zeros((HG_ROWS, LANES), F32)

    a = jnp.zeros((c, c), F32)
    level, mv = 1, nv // 2
    while mv >= 1:
        qm = [zero] * nv
        km = [zero] * nv
        for g in range(nv // (2 * mv)):
            ref = bnd[2 * mv * g + mv - 1]
            for p in range(2 * mv * g, 2 * mv * g + mv):
                km[p] = jnp.exp2(lk_p[p] + (ref - b_p[p]))
            for p in range(2 * mv * g + mv, 2 * mv * (g + 1)):
                qm[p] = q_p[p] * jnp.exp2(b_p[p] - ref)
        yield
        al = _dot_nt(assemble(qm).astype(BF16), assemble(km).astype(BF16))
        a = jnp.where(lvl == float(level), al, a)
        level += 1
        mv //= 2

    rib = lax.broadcasted_iota(jnp.int32, (HG_ROWS, LANES), 0)
    slabs = []
    for s in range(HG_ROWS):
        valid = (rib <= s) if reverse else (rib >= s)
        col = []
        for p in range(nv):
            cs = jnp.broadcast_to((b_p[p] - lk_p[p])[s:s + 1, :], (HG_ROWS, LANES))
            col.append(q_p[p] * jnp.exp2(jnp.where(valid, b_p[p] - cs, NEG)))
        slabs.append(assemble(col).astype(BF16))
        yield
    adiag = _dot(jnp.concatenate(slabs, axis=1), emat)
    a = jnp.where(lvl == float(level), adiag, a)

    tot = bnd[nv - 1]
    qbar = assemble([q_p[p] * jnp.exp2(b_p[p]) for p in range(nv)]).astype(BF16)
    kbar = assemble([jnp.exp2(lk_p[p] + (tot - b_p[p])) for p in range(nv)]).astype(BF16)
    vb = v.astype(BF16)
    yield
    st = st_ref[...]
    o = _dot(a.astype(BF16), vb) + _dot_nt(qbar, st.astype(BF16))
    st_ref[...] = jnp.exp2(tot[0:1, :]) * st + _dot_tn(vb, kbar)
    return o


def _lockstep(gens):
    results = [None] * len(gens)
    live = list(range(len(gens)))
    while live:
        for idx in list(live):
            try:
                next(gens[idx])
            except StopIteration as done:
                results[idx] = done.value
                live.remove(idx)
    return results


def _hgrn_kernel(qf_ref, zf_ref, vf_ref, gf_ref, qb_ref, zb_ref, vb_ref, gb_ref,
                 crow_ref, cmat_ref, lvl_ref, emat_ref, o_ref,
                 part_ref, stf_ref, stb_ref, *, step, chunk, nsteps):
    c = pl.program_id(2)

    @pl.when(c == 0)
    def _():
        stf_ref[...] = jnp.zeros_like(stf_ref)
        stb_ref[...] = jnp.zeros_like(stb_ref)
        part_ref[...] = jnp.zeros_like(part_ref)

    crow = crow_ref[0]
    gain = crow[4:5, :]
    emat = emat_ref[...]
    nch = step // chunk

    def emit(o, row, g):
        t = o + part_ref[pl.ds(row, chunk), :]
        part_ref[pl.ds(row, chunk), :] = t
        o_ref[pl.ds(row, chunk), :] = (_rms(t, gain) * (g * _sigmoid(g))).astype(BF16)

    def fwd(j):
        rf = pl.multiple_of(j * chunk, chunk)
        sl = pl.ds(rf, chunk)
        gen = _hgrn_chunk(qf_ref[sl, :], zf_ref[sl, :], vf_ref[sl, :], cmat_ref[0], lvl_ref[0], emat,
                          crow[0:1, :], crow[1:2, :], stf_ref, False)
        return gen, pl.multiple_of(c * step + rf, chunk), gf_ref, sl

    def bwd(j):
        rb = pl.multiple_of((nch - 1 - j) * chunk, chunk)
        sl = pl.ds(rb, chunk)
        gen = _hgrn_chunk(qb_ref[sl, :], zb_ref[sl, :], vb_ref[sl, :], cmat_ref[1], lvl_ref[1], emat,
                          crow[2:3, :], crow[3:4, :], stb_ref, True)
        return gen, pl.multiple_of((nsteps - 1 - c) * step + rb, chunk), gb_ref, sl

    unroll = math.gcd(HG_UNROLL, nch)

    def body(i, carry):
        chains = [f(i * unroll + u) for u in range(unroll) for f in (fwd, bwd)]
        outs = _lockstep([ch[0] for ch in chains])
        for o, (_, row, g_ref, sl) in zip(outs, chains):
            emit(o, row, g_ref[sl, :])
        return carry

    lax.fori_loop(0, nch // unroll, body, 0)


def _hgrn_consts(chunk):
    t = np.arange(chunk)[:, None]
    s = np.arange(chunk)[None, :]
    cm = np.stack([(s <= t), (s >= t)]).astype(np.float32)
    nlev = int(round(math.log2(chunk // HG_ROWS)))
    lv = np.zeros((chunk, chunk), np.float32)
    for level in range(1, nlev + 1):
        m = chunk >> level
        hit = (t // (2 * m) == s // (2 * m)) & (t // m != s // m) & (s < t)
        lv[hit] = level
    lv[(t // HG_ROWS) == (s // HG_ROWS)] = nlev + 1
    off = (lv >= 1) & (lv <= nlev)
    lvb = np.where(off.T, lv.T, 0.0)
    lvb[(t // HG_ROWS) == (s // HG_ROWS)] = nlev + 1
    lvl = np.stack([lv, lvb]).astype(np.float32)
    em = (np.arange(HG_ROWS * LANES)[:, None] // LANES == (np.arange(chunk)[None, :] % HG_ROWS))
    return jnp.asarray(cm, BF16), jnp.asarray(lvl, F32), jnp.asarray(em.astype(np.float32), BF16)


def _hgrn(z, crow, nb, seq):
    m = z.shape[0]
    step = _pick(seq // 2, HG_STEP, 256, 128, 64)
    chunk = HG_CHUNK
    n = seq // step
    half = n // 2
    cmat, lvl, emat = _hgrn_consts(chunk)

    def zspec(col, fwd, gate=False):
        if fwd:
            blk = (lambda c: jnp.maximum(c, half)) if gate else (lambda c: c)
        else:
            blk = (lambda c: jnp.minimum(n - 1 - c, half - 1)) if gate else (lambda c: n - 1 - c)
        return pl.BlockSpec((step, LANES), lambda b, h, c: (b * n + blk(c), col // LANES + h))

    const3 = lambda shape: pl.BlockSpec(shape, lambda b, h, c: (0, 0, 0))
    return pl.pallas_call(
        functools.partial(_hgrn_kernel, step=step, chunk=chunk, nsteps=n),
        grid=(nb, HEADS, n),
        in_specs=[
            zspec(OFF_Q, True), zspec(OFF_FF, True), zspec(OFF_I, True), zspec(OFF_G, True, True),
            zspec(OFF_Q, False), zspec(OFF_FB, False), zspec(OFF_I, False), zspec(OFF_G, False, True),
            pl.BlockSpec((1, SUBLANES, LANES), lambda b, h, c: (h, 0, 0)),
            const3(cmat.shape), const3(lvl.shape),
            pl.BlockSpec(emat.shape, lambda b, h, c: (0, 0)),
        ],
        out_specs=pl.BlockSpec((seq, LANES), lambda b, h, c: (b, h)),
        out_shape=jax.ShapeDtypeStruct((m, HG_WIDTH), BF16),
        scratch_shapes=[
            pltpu.VMEM((seq, LANES), F32),
            pltpu.VMEM((HEAD_DIM, HEAD_DIM), F32),
            pltpu.VMEM((HEAD_DIM, HEAD_DIM), F32),
        ],
        compiler_params=_params("parallel", "parallel", "arbitrary"),
        name="hgrn2",
    )(z, z, z, z, z, z, z, z, crow, cmat, lvl, emat)


def _qproj_kernel(z_ref, g_ref, w_ref, cos_ref, sin_ref, o_ref):
    xn = _rms(z_ref[...], g_ref[...]).astype(BF16)
    cz = cos_ref[...]
    sz = sin_ref[...]
    for h in range(HEADS):
        acc = _dot(xn, w_ref[:, h * 3 * LANES:(h + 1) * 3 * LANES])
        o_ref[:, h * Q_PAD:h * Q_PAD + LANES] = (acc[:, :LANES] * QK_SCALE).astype(BF16)
        rope = acc[:, LANES:2 * LANES] * cz + acc[:, 2 * LANES:] * sz
        o_ref[:, h * Q_PAD + LANES:(h + 1) * Q_PAD] = (rope * QK_SCALE).astype(BF16)


def _qproj(z, g, w, cosz, sinz, seq):
    m = z.shape[0]
    tm = _pick(seq, 512, 256, 128)
    ns = seq // tm
    return pl.pallas_call(
        _qproj_kernel,
        grid=(m // tm,),
        in_specs=[
            pl.BlockSpec((tm, LORA), lambda i: (i, OFF_QA // LORA)),
            pl.BlockSpec((1, LORA), lambda i: (0, 0)),
            pl.BlockSpec(w.shape, lambda i: (0, 0)),
            pl.BlockSpec((tm, LANES), lambda i: (i % ns, 0)),
            pl.BlockSpec((tm, LANES), lambda i: (i % ns, 0)),
        ],
        out_specs=pl.BlockSpec((tm, HEADS * Q_PAD), lambda i: (i, 0)),
        out_shape=jax.ShapeDtypeStruct((m, HEADS * Q_PAD), BF16),
        compiler_params=_params("parallel"),
        name="mla_q_proj",
    )(z, g.reshape(1, LORA), w, cosz, sinz)


def _kvproj_kernel(z_ref, kr_ref, g_ref, wk_ref, wvt_ref, cos_ref, sin_ref, k_ref, vt_ref):
    xn = _rms(z_ref[...], g_ref[...]).astype(BF16)
    kr = kr_ref[...]
    krope = (kr[:, :LANES] * cos_ref[...] + kr[:, LANES:] * sin_ref[...]).astype(BF16)
    kn = _dot(xn, wk_ref[...])
    for h in range(HEADS):
        k_ref[:, h * Q_PAD:h * Q_PAD + LANES] = kn[:, h * LANES:(h + 1) * LANES].astype(BF16)
        k_ref[:, h * Q_PAD + LANES:(h + 1) * Q_PAD] = krope
    vt = _dot_nt(wvt_ref[...], xn).astype(BF16)
    ones = jnp.ones((VT_ROWS - HEAD_DIM, vt.shape[1]), BF16)
    for h in range(HEADS):
        vt_ref[0, 0, h, :HEAD_DIM, :] = vt[h * HEAD_DIM:(h + 1) * HEAD_DIM, :]
        vt_ref[0, 0, h, HEAD_DIM:, :] = ones


def _kvproj(z, zkr, g, wk, wvt, cosz, sinz, nb, seq, tk):
    m = z.shape[0]
    ns = seq // tk
    return pl.pallas_call(
        _kvproj_kernel,
        grid=(m // tk,),
        in_specs=[
            pl.BlockSpec((tk, LORA), lambda i: (i, OFF_KVA // LORA)),
            pl.BlockSpec((tk, 2 * LANES), lambda i: (i, 0)),
            pl.BlockSpec((1, LORA), lambda i: (0, 0)),
            pl.BlockSpec(wk.shape, lambda i: (0, 0)),
            pl.BlockSpec(wvt.shape, lambda i: (0, 0)),
            pl.BlockSpec((tk, LANES), lambda i: (i % ns, 0)),
            pl.BlockSpec((tk, LANES), lambda i: (i % ns, 0)),
        ],
        out_specs=[
            pl.BlockSpec((tk, HEADS * Q_PAD), lambda i: (i, 0)),
            pl.BlockSpec((1, 1, HEADS, VT_ROWS, tk), lambda i: (i // ns, i % ns, 0, 0, 0)),
        ],
        out_shape=[
            jax.ShapeDtypeStruct((m, HEADS * Q_PAD), BF16),
            jax.ShapeDtypeStruct((nb, ns, HEADS, VT_ROWS, tk), BF16),
        ],
        compiler_params=_params("parallel"),
        name="mla_kv_proj",
    )(z, zkr, g.reshape(1, LORA), wk, wvt, cosz, sinz)


def _col_reduce(x, op, slab=64):
    r, c = x.shape
    if r > slab and r % slab == 0:
        x = op(x.reshape(r // slab, slab, c), axis=0)
    return op(x, axis=0, keepdims=True)


def _attn_kernel(q_ref, k_ref, vt_ref, o_ref, acc_ref, st_ref, mx_ref, *, tk, nk, nblk):
    tq = q_ref.shape[0]
    nslot, ng, _, gw = st_ref.shape
    assert nslot == nblk
    acc_ref[...] = jnp.zeros_like(acc_ref)

    def scores(j, slot, g):
        kb = k_ref[pl.ds(pl.multiple_of(j * tk, tk), tk), :]
        st = _dot_nt(kb, q_ref[g * gw:(g + 1) * gw, :])
        st_ref[slot, g] = st
        mx_ref[slot, g] = _col_reduce(st, jnp.max)

    def consume(j, slot, g, m):
        cols = slice(g * gw, (g + 1) * gw)
        m_new = jnp.maximum(m, mx_ref[slot, g])
        alpha = jnp.exp2(m - m_new)
        p = jnp.exp2((st_ref[slot, g] - m_new).astype(BF16))
        acc_ref[:, cols] = alpha * acc_ref[:, cols] + _dot(vt_ref[0, j, 0], p)
        return m_new

    def body(i, ms):
        j = nblk * i
        ms = list(ms)
        for u in range(nblk):
            nxt = j + u + 1
            if u == nblk - 1:
                nxt = jnp.minimum(nxt, nk - 1)
            for g in range(ng):
                scores(nxt, (u + 1) % nslot, g)
                ms[g] = consume(j + u, u % nslot, g, ms[g])
        return tuple(ms)

    for g in range(ng):
        scores(0, 0, g)
    init = tuple(jnp.full((1, gw), NEG, F32) for _ in range(ng))
    lax.fori_loop(0, nk // nblk, body, init)
    for g in range(ng):
        cols = slice(g * gw, (g + 1) * gw)
        o_ref[cols, :] = (acc_ref[:HEAD_DIM, cols] / acc_ref[HEAD_DIM:HEAD_DIM + 1, cols]).T.astype(BF16)


def _attention(qp, kp, vt, nb, seq, tk):
    m = qp.shape[0]
    tq = _pick(seq, 1024, 512, 256, 128)
    nq = seq // tq
    nk = seq // tk
    nblk = ATT_BLOCKS if nk % ATT_BLOCKS == 0 else 2
    assert nk % nblk == 0, "the key loop handles an even number of blocks per trip"
    gw = min(ATT_GROUP, tq)
    return pl.pallas_call(
        functools.partial(_attn_kernel, tk=tk, nk=nk, nblk=nblk),
        grid=(nb, HEADS, nq),
        in_specs=[
            pl.BlockSpec((tq, Q_PAD), lambda b, h, i: (b * nq + i, h)),
            pl.BlockSpec((seq, Q_PAD), lambda b, h, i: (b, h)),
            pl.BlockSpec((1, nk, 1, VT_ROWS, tk), lambda b, h, i: (b, 0, h, 0, 0)),
        ],
        out_specs=pl.BlockSpec((tq, HEAD_DIM), lambda b, h, i: (b * nq + i, h)),
        out_shape=jax.ShapeDtypeStruct((m, HG_WIDTH), BF16),
        scratch_shapes=[pltpu.VMEM((VT_ROWS, tq), F32), pltpu.VMEM((nblk, tq // gw, tk, gw), F32),
                        pltpu.VMEM((nblk, tq // gw, 1, gw), F32)],
        compiler_params=_params("parallel", "parallel", "arbitrary"),
        name="mla_attention",
    )(qp, kp, vt)


def _mm_res_kernel(*refs, n_in):
    res_ref, o_ref = refs[2 * n_in], refs[2 * n_in + 1]
    acc = res_ref[...]
    for a_ref, w_ref in zip(refs[:n_in], refs[n_in:2 * n_in]):
        acc = acc + _dot(a_ref[...], w_ref[...])
    o_ref[...] = acc


def _mm_res(a_list, w, res, tm, tn):
    m, n = res.shape
    n_in = len(a_list)
    kb = a_list[0].shape[1]
    assert all(a.shape[1] == kb for a in a_list) and w.shape[0] == n_in * kb
    w_list = [w] * n_in
    in_specs = [pl.BlockSpec((tm, kb), lambda i, j: (i, 0)) for _ in a_list]
    in_specs += [pl.BlockSpec((kb, tn), functools.partial(lambda i, j, r: (r, j), r=r))
                 for r in range(n_in)]
    in_specs += [pl.BlockSpec((tm, tn), lambda i, j: (i, j))]
    return pl.pallas_call(
        functools.partial(_mm_res_kernel, n_in=n_in),
        grid=(m // tm, n // tn),
        in_specs=in_specs,
        out_specs=pl.BlockSpec((tm, tn), lambda i, j: (i, j)),
        out_shape=jax.ShapeDtypeStruct((m, n), F32),
        compiler_params=_params("parallel", "arbitrary"),
        name="mm_residual",
    )(*a_list, *w_list, res)


HALO = 16


def _ffn_up_kernel(x_ref, xp_ref, xn_ref, g_ref, wg_ref, wu_ref, cw_ref, cb_ref, o_ref, n_ref,
                   *, tm, tiles_per_seq):
    i = pl.program_id(0)

    @pl.when(pl.program_id(1) == 0)
    def _():
        g = g_ref[...]
        first = (i % tiles_per_seq) == 0
        last = (i % tiles_per_seq) == tiles_per_seq - 1
        n_ref[0:HALO, :] = jnp.where(first, 0.0, _rms(xp_ref[...], g)).astype(BF16)
        n_ref[HALO:HALO + tm, :] = _rms(x_ref[...], g).astype(BF16)
        n_ref[HALO + tm:, :] = jnp.where(last, 0.0, _rms(xn_ref[...], g)).astype(BF16)

    ge = _dot(n_ref[...], wg_ref[...])
    rows = tm + 2 * HALO
    prev = pltpu.roll(ge, 1, 0)[HALO:HALO + tm, :]
    nxt = pltpu.roll(ge, rows - 1, 0)[HALO:HALO + tm, :]
    cw = cw_ref[...]
    gate = prev * cw[0:1, :] + ge[HALO:HALO + tm, :] * cw[1:2, :] + nxt * cw[2:3, :] + cb_ref[...]
    up = _dot(n_ref[HALO:HALO + tm, :], wu_ref[...])
    o_ref[...] = (gate * _sigmoid(gate) * up).astype(BF16)


def _ffn_up(h, g, w_up, conv_w, conv_b, seq):
    m, k = h.shape
    tm = _pick(seq, 1024, 512, 256, 128)
    tf = 512
    nf = D_FF // tf
    tps = seq // tm
    r = tm // HALO
    nblk = m // HALO
    return pl.pallas_call(
        functools.partial(_ffn_up_kernel, tm=tm, tiles_per_seq=tps),
        grid=(m // tm, nf),
        in_specs=[
            pl.BlockSpec((tm, k), lambda i, j: (i, 0)),
            pl.BlockSpec((HALO, k), lambda i, j: (jnp.maximum(i * r - 1, 0), 0)),
            pl.BlockSpec((HALO, k), lambda i, j: (jnp.minimum((i + 1) * r, nblk - 1), 0)),
            pl.BlockSpec((1, k), lambda i, j: (0, 0)),
            pl.BlockSpec((k, tf), lambda i, j: (0, j)),
            pl.BlockSpec((k, tf), lambda i, j: (0, j + nf)),
            pl.BlockSpec((3, tf), lambda i, j: (0, j)),
            pl.BlockSpec((1, tf), lambda i, j: (0, j)),
        ],
        out_specs=pl.BlockSpec((tm, tf), lambda i, j: (i, j)),
        out_shape=jax.ShapeDtypeStruct((m, D_FF), BF16),
        scratch_shapes=[pltpu.VMEM((tm + 2 * HALO, k), BF16)],
        compiler_params=_params("parallel", "arbitrary"),
        name="ffn_up",
    )(h, h, h, g.reshape(1, k), w_up, w_up, conv_w, conv_b.reshape(1, D_FF))


def _ple_kernel(h_ref, p_ref, g_ref, wg_ref, wp_ref, fn_ref, o_ref, *, final, tn):
    x = h_ref[...]
    xn = _rms(x, g_ref[...]).astype(BF16)
    pb = p_ref[...].astype(BF16)
    for c in range(D_MODEL // tn):
        sl = slice(c * tn, (c + 1) * tn)
        gate = _dot(xn, wg_ref[:, sl])
        o_ref[:, sl] = x[:, sl] + _dot(pb, wp_ref[:, sl]) * _sigmoid(gate)
    if final:
        o_ref[...] = _rms(o_ref[...], fn_ref[...])


def _ple(h, p, g, wg, wp, fn, final):
    m, k = h.shape
    tm = _pick(m, 512, 256, 128)
    return pl.pallas_call(
        functools.partial(_ple_kernel, final=final, tn=512),
        grid=(m // tm,),
        in_specs=[
            pl.BlockSpec((tm, k), lambda i: (i, 0)),
            pl.BlockSpec((tm, PLE_DIM), lambda i: (i, 0)),
            pl.BlockSpec((1, k), lambda i: (0, 0)),
            pl.BlockSpec(wg.shape, lambda i: (0, 0)),
            pl.BlockSpec(wp.shape, lambda i: (0, 0)),
            pl.BlockSpec((1, k), lambda i: (0, 0)),
        ],
        out_specs=pl.BlockSpec((tm, k), lambda i: (i, 0)),
        out_shape=jax.ShapeDtypeStruct((m, k), F32),
        compiler_params=_params("parallel"),
        name="ple_gate",
    )(h, p, g.reshape(1, k), wg, wp, fn.reshape(1, k))


def _rot_cols(w):
    half = ROPE_DIM // 2
    return jnp.concatenate([-w[..., half:], w[..., :half]], axis=-1)


def _pad_rope(w):
    return jnp.concatenate([w, jnp.zeros_like(w)], axis=-1)


def _prepare(attn_norm, w_in, hg_lower, hg_norm, w_q_b, w_kv_b, w_out, w_up, w_down,
             w_ple_gate, w_ple_proj):
    w_kr = w_in[:, :, OFF_KR:]
    wq = w_q_b.reshape(DEPTH, LORA, HEADS, HEAD_DIM + ROPE_DIM)
    wq_r = wq[..., HEAD_DIM:]
    wq = jnp.concatenate([wq[..., :HEAD_DIM], _pad_rope(wq_r), _pad_rope(_rot_cols(wq_r))], axis=-1)
    wkv = w_kv_b.reshape(DEPTH, LORA, HEADS, 2 * HEAD_DIM)

    lb = jnp.cumsum(jax.nn.softmax(hg_lower.astype(F32), axis=1), axis=1)
    lb = jnp.maximum(lb - lb[:, :1], 0.0)
    loglb = jnp.maximum(jnp.log(lb) * LOG2E, NEG).reshape(2, DEPTH, HEADS, HEAD_DIM)
    log1m = (jnp.log1p(-lb) * LOG2E).reshape(2, DEPTH, HEADS, HEAD_DIM)
    gain = hg_norm.astype(F32).reshape(DEPTH, HEADS, HEAD_DIM)
    zeros = jnp.zeros_like(gain)
    crow = jnp.stack([loglb[0], log1m[0], loglb[1], log1m[1], gain, zeros, zeros, zeros], axis=2)

    return dict(
        w_in=w_in[:, :, :OFF_KR].astype(BF16),
        w_kr=jnp.concatenate([_pad_rope(w_kr), _pad_rope(_rot_cols(w_kr))], axis=-1).astype(BF16),
        w_q=wq.reshape(DEPTH, LORA, HEADS * 3 * LANES).astype(BF16),
        w_k=wkv[..., :HEAD_DIM].reshape(DEPTH, LORA, HG_WIDTH).astype(BF16),
        w_vt=jnp.swapaxes(wkv[..., HEAD_DIM:].reshape(DEPTH, LORA, HG_WIDTH), 1, 2).astype(BF16),
        w_out=w_out.astype(BF16),
        w_up=w_up.astype(BF16),
        w_down=w_down.astype(BF16),
        w_ple_gate=w_ple_gate.astype(BF16),
        w_ple_proj=w_ple_proj.astype(BF16),
        crow=crow,
    )


def _rope_tables(seq):
    inv = ROPE_THETA ** (-jnp.arange(0, ROPE_DIM, 2, dtype=F32) / ROPE_DIM)
    ang = jnp.arange(seq, dtype=F32)[:, None] * inv[None, :]
    pad = jnp.zeros((seq, LANES - ROPE_DIM), F32)
    cosz = jnp.concatenate([jnp.cos(ang), jnp.cos(ang), pad], axis=-1)
    sinz = jnp.concatenate([jnp.sin(ang), jnp.sin(ang), pad], axis=-1)
    return cosz, sinz


def _trunk(x, p, attn_norm, q_a_norm, kv_a_norm, ffn_norm, conv_w, conv_b, ple_norm, final_norm, prm):
    nb, seq, d = x.shape
    m = nb * seq
    h = x.reshape(m, d)
    p = p.reshape(DEPTH, m, PLE_DIM)
    cosz, sinz = _rope_tables(seq)
    tm = _pick(seq, 1024, 512, 256, 128)
    tk = _pick(seq, 512, 256, 128)
    for l in range(DEPTH):
        z = _norm_mm(h, attn_norm[l], prm["w_in"][l], tm, 1024)
        zkr = _norm_mm(h, attn_norm[l], prm["w_kr"][l], tm, 2 * LANES)
        o_hg = _hgrn(z, prm["crow"][l], nb, seq)
        qp = _qproj(z, q_a_norm[l], prm["w_q"][l], cosz, sinz, seq)
        kp, vt = _kvproj(z, zkr, kv_a_norm[l], prm["w_k"][l], prm["w_vt"][l], cosz, sinz, nb, seq, tk)
        o_mla = _attention(qp, kp, vt, nb, seq, tk)
        h = _mm_res([o_hg, o_mla], prm["w_out"][l], h, tm, 512)
        act = _ffn_up(h, ffn_norm[l], prm["w_up"][l], conv_w[l], conv_b[l], seq)
        h = _mm_res([act], prm["w_down"][l], h, tm, 512)
        h = _ple(h, p[l], ple_norm[l], prm["w_ple_gate"][l], prm["w_ple_proj"][l], final_norm,
                 final=(l == DEPTH - 1))
    return h.reshape(nb, seq, d)


def kernel(x_prompt, x_sample, p_prompt, p_sample, attn_norm, w_in, hg_lower, hg_norm, q_a_norm,
           w_q_b, kv_a_norm, w_kv_b, w_out, ffn_norm, w_up, conv_w, conv_b, w_down, ple_norm,
           w_ple_gate, w_ple_proj, final_norm):
    assert x_prompt.shape[1:] == x_sample.shape[1:], "both groups must share the sequence length"
    nbp = x_prompt.shape[0]
    prm = _prepare(attn_norm, w_in, hg_lower, hg_norm, w_q_b, w_kv_b, w_out, w_up, w_down,
                   w_ple_gate, w_ple_proj)
    x = jnp.concatenate([x_prompt, x_sample], axis=0)
    p = jnp.concatenate([p_prompt, p_sample], axis=1)
    y = _trunk(x, p, attn_norm, q_a_norm, kv_a_norm, ffn_norm, conv_w, conv_b, ple_norm, final_norm, prm)
    return (y[:nbp], y[nbp:])
```

```python
import functools
import math

import jax
import jax.numpy as jnp
import numpy as np
from jax import lax
from jax.experimental import pallas as pl
from jax.experimental.pallas import tpu as pltpu

F32 = jnp.float32
BF16 = jnp.bfloat16

D_MODEL = 2048
DEPTH = 4
HEADS = 8
HEAD_DIM = 128
ROPE_DIM = 64
LORA = 512
HG_WIDTH = HEADS * HEAD_DIM
OFF_Q, OFF_FF, OFF_FB, OFF_I, OFF_G, OFF_QA, OFF_KVA, OFF_KR = (
    0, 1024, 2048, 3072, 4096, 5120, 5632, 6144)
D_FF = 5632
PLE_DIM = 256
EPS = 1e-6
ROPE_THETA = 10000.0
NEG = -1e30

LANES = 128
SUBLANES = 8
VMEM_LIMIT = 56 * 2**20

HG_CHUNK = 64
HG_ROWS = SUBLANES
HG_STEP = 512
HG_UNROLL = 8
Q_PAD = 2 * LANES
LOG2E = math.log2(math.e)
QK_SCALE = (HEAD_DIM + ROPE_DIM) ** -0.5 * LOG2E
ATT_GROUP = 2 * LANES
ATT_BLOCKS = 8
VT_ROWS = HEAD_DIM + 16


def _params(*sem):
    return pltpu.CompilerParams(dimension_semantics=sem, vmem_limit_bytes=VMEM_LIMIT)


def _pick(n, *cands):
    for c in cands:
        if n % c == 0:
            return c
    raise ValueError(f"no tile for {n} in {cands}")


def _rms(x, g):
    ms = jnp.mean(x * x, axis=-1, keepdims=True)
    return x * lax.rsqrt(ms + EPS) * g


def _sigmoid(x):
    return 1.0 / (1.0 + jnp.exp(-x))


def _dot(a, b):
    return jnp.dot(a, b, preferred_element_type=F32)


def _dot_nt(a, b):
    return lax.dot_general(a, b, (((1,), (1,)), ((), ())), preferred_element_type=F32)


def _dot_tn(a, b):
    return lax.dot_general(a, b, (((0,), (0,)), ((), ())), preferred_element_type=F32)


def _norm_mm_kernel(x_ref, g_ref, w_ref, ws_ref, o_ref, os_ref, xn_ref):
    @pl.when(pl.program_id(1) == 0)
    def _():
        xn_ref[...] = _rms(x_ref[...], g_ref[...]).astype(BF16)
        os_ref[...] = _dot(xn_ref[...], ws_ref[...])

    o_ref[...] = _dot(xn_ref[...], w_ref[...])


def _norm_mm(x, g, w, w_side, tm, tn):
    m, k = x.shape
    n = w.shape[1]
    ns = w_side.shape[1]
    return pl.pallas_call(
        _norm_mm_kernel,
        grid=(m // tm, n // tn),
        in_specs=[
            pl.BlockSpec((tm, k), lambda i, j: (i, 0)),
            pl.BlockSpec((1, k), lambda i, j: (0, 0)),
            pl.BlockSpec((k, tn), lambda i, j: (0, j)),
            pl.BlockSpec((k, ns), lambda i, j: (0, 0)),
        ],
        out_specs=[pl.BlockSpec((tm, tn), lambda i, j: (i, j)),
                   pl.BlockSpec((tm, ns), lambda i, j: (i, 0))],
        out_shape=[jax.ShapeDtypeStruct((m, n), F32), jax.ShapeDtypeStruct((m, ns), F32)],
        scratch_shapes=[pltpu.VMEM((tm, k), BF16)],
        compiler_params=_params("parallel", "arbitrary"),
        name="norm_mm",
    )(x, g.reshape(1, k), w, w_side)


def _hgrn_gates(z, loglb, log1m):
    z2 = z * LOG2E
    l1p = jnp.log2(1.0 + jnp.exp2(-jnp.abs(z2)))
    c = log1m + (jnp.minimum(z2, 0.0) - l1p)
    logf = jnp.maximum(loglb, c) + jnp.log2(1.0 + jnp.exp2(-jnp.abs(loglb - c)))
    return logf, c - z2


def _hgrn_chunk(q, zf, v, cmat, lvl, emat, loglb, log1m, st_ref, reverse):
    c = q.shape[0]
    nv = c // HG_ROWS
    logf, logk = _hgrn_gates(zf, loglb, log1m)

    hi = logf.astype(BF16)
    lo = (logf - hi.astype(F32)).astype(BF16)
    yield
    bb = _dot(cmat, jnp.concatenate([hi, lo], axis=1))
    b = bb[:, :LANES] + bb[:, LANES:]

    def rows(x, p):
        i = nv - 1 - p if reverse else p
        return x[i * HG_ROWS:(i + 1) * HG_ROWS, :]

    def assemble(groups):
        return jnp.concatenate(groups[::-1] if reverse else groups, axis=0)

    edge = 0 if reverse else HG_ROWS - 1
    b_p = [rows(b, p) for p in range(nv)]
    q_p = [rows(q, p) for p in range(nv)]
    lk_p = [rows(logk, p) for p in range(nv)]
    bnd = [jnp.broadcast_to(x[edge:edge + 1, :], (HG_ROWS, LANES)) for x in b_p]
    zero = jnp.zeros((HG_ROWS, LANES), F32)

    a = jnp.zeros((c, c), F32)
    level, mv = 1, nv // 2
    while mv >= 1:
        qm = [zero] * nv
        km = [zero] * nv
        for g in range(nv // (2 * mv)):
            ref = bnd[2 * mv * g + mv - 1]
            for p in range(2 * mv * g, 2 * mv * g + mv):
                km[p] = jnp.exp2(lk_p[p] + (ref - b_p[p]))
            for p in range(2 * mv * g + mv, 2 * mv * (g + 1)):
                qm[p] = q_p[p] * jnp.exp2(b_p[p] - ref)
        yield
        al = _dot_nt(assemble(qm).astype(BF16), assemble(km).astype(BF16))
        a = jnp.where(lvl == float(level), al, a)
        level += 1
        mv //= 2

    rib = lax.broadcasted_iota(jnp.int32, (HG_ROWS, LANES), 0)
    slabs = []
    for s in range(HG_ROWS):
        valid = (rib <= s) if reverse else (rib >= s)
        col = []
        for p in range(nv):
            cs = jnp.broadcast_to((b_p[p] - lk_p[p])[s:s + 1, :], (HG_ROWS, LANES))
            col.append(q_p[p] * jnp.exp2(jnp.where(valid, b_p[p] - cs, NEG)))
        slabs.append(assemble(col).astype(BF16))
        yield
    adiag = _dot(jnp.concatenate(slabs, axis=1), emat)
    a = jnp.where(lvl == float(level), adiag, a)

    tot = bnd[nv - 1]
    qbar = assemble([q_p[p] * jnp.exp2(b_p[p]) for p in range(nv)]).astype(BF16)
    kbar = assemble([jnp.exp2(lk_p[p] + (tot - b_p[p])) for p in range(nv)]).astype(BF16)
    vb = v.astype(BF16)
    yield
    st = st_ref[...]
    o = _dot(a.astype(BF16), vb) + _dot_nt(qbar, st.astype(BF16))
    st_ref[...] = jnp.exp2(tot[0:1, :]) * st + _dot_tn(vb, kbar)
    return o


def _lockstep(gens):
    results = [None] * len(gens)
    live = list(range(len(gens)))
    while live:
        for idx in list(live):
            try:
                next(gens[idx])
            except StopIteration as done:
                results[idx] = done.value
                live.remove(idx)
    return results


def _hgrn_kernel(qf_ref, zf_ref, vf_ref, gf_ref, qb_ref, zb_ref, vb_ref, gb_ref,
                 crow_ref, cmat_ref, lvl_ref, emat_ref, o_ref,
                 part_ref, stf_ref, stb_ref, *, step, chunk, nsteps):
    c = pl.program_id(2)

    @pl.when(c == 0)
    def _():
        stf_ref[...] = jnp.zeros_like(stf_ref)
        stb_ref[...] = jnp.zeros_like(stb_ref)
        part_ref[...] = jnp.zeros_like(part_ref)

    crow = crow_ref[0]
    gain = crow[4:5, :]
    emat = emat_ref[...]
    nch = step // chunk

    def emit(o, row, g):
        t = o + part_ref[pl.ds(row, chunk), :]
        part_ref[pl.ds(row, chunk), :] = t
        o_ref[pl.ds(row, chunk), :] = (_rms(t, gain) * (g * _sigmoid(g))).astype(BF16)

    def fwd(j):
        rf = pl.multiple_of(j * chunk, chunk)
        sl = pl.ds(rf, chunk)
        gen = _hgrn_chunk(qf_ref[sl, :], zf_ref[sl, :], vf_ref[sl, :], cmat_ref[0], lvl_ref[0], emat,
                          crow[0:1, :], crow[1:2, :], stf_ref, False)
        return gen, pl.multiple_of(c * step + rf, chunk), gf_ref, sl

    def bwd(j):
        rb = pl.multiple_of((nch - 1 - j) * chunk, chunk)
        sl = pl.ds(rb, chunk)
        gen = _hgrn_chunk(qb_ref[sl, :], zb_ref[sl, :], vb_ref[sl, :], cmat_ref[1], lvl_ref[1], emat,
                          crow[2:3, :], crow[3:4, :], stb_ref, True)
        return gen, pl.multiple_of((nsteps - 1 - c) * step + rb, chunk), gb_ref, sl

    unroll = math.gcd(HG_UNROLL, nch)

    def body(i, carry):
        chains = [f(i * unroll + u) for u in range(unroll) for f in (fwd, bwd)]
        outs = _lockstep([ch[0] for ch in chains])
        for o, (_, row, g_ref, sl) in zip(outs, chains):
            emit(o, row, g_ref[sl, :])
        return carry

    lax.fori_loop(0, nch // unroll, body, 0)


def _hgrn_consts(chunk):
    t = np.arange(chunk)[:, None]
    s = np.arange(chunk)[None, :]
    cm = np.stack([(s <= t), (s >= t)]).astype(np.float32)
    nlev = int(round(math.log2(chunk // HG_ROWS)))
    lv = np.zeros((chunk, chunk), np.float32)
    for level in range(1, nlev + 1):
        m = chunk >> level
        hit = (t // (2 * m) == s // (2 * m)) & (t // m != s // m) & (s < t)
        lv[hit] = level
    lv[(t // HG_ROWS) == (s // HG_ROWS)] = nlev + 1
    off = (lv >= 1) & (lv <= nlev)
    lvb = np.where(off.T, lv.T, 0.0)
    lvb[(t // HG_ROWS) == (s // HG_ROWS)] = nlev + 1
    lvl = np.stack([lv, lvb]).astype(np.float32)
    em = (np.arange(HG_ROWS * LANES)[:, None] // LANES == (np.arange(chunk)[None, :] % HG_ROWS))
    return jnp.asarray(cm, BF16), jnp.asarray(lvl, F32), jnp.asarray(em.astype(np.float32), BF16)


def _hgrn(z, crow, nb, seq):
    m = z.shape[0]
    step = _pick(seq // 2, HG_STEP, 256, 128, 64)
    chunk = HG_CHUNK
    n = seq // step
    half = n // 2
    cmat, lvl, emat = _hgrn_consts(chunk)

    def zspec(col, fwd, gate=False):
        if fwd:
            blk = (lambda c: jnp.maximum(c, half)) if gate else (lambda c: c)
        else:
            blk = (lambda c: jnp.minimum(n - 1 - c, half - 1)) if gate else (lambda c: n - 1 - c)
        return pl.BlockSpec((step, LANES), lambda b, h, c: (b * n + blk(c), col // LANES + h))

    const3 = lambda shape: pl.BlockSpec(shape, lambda b, h, c: (0, 0, 0))
    return pl.pallas_call(
        functools.partial(_hgrn_kernel, step=step, chunk=chunk, nsteps=n),
        grid=(nb, HEADS, n),
        in_specs=[
            zspec(OFF_Q, True), zspec(OFF_FF, True), zspec(OFF_I, True), zspec(OFF_G, True, True),
            zspec(OFF_Q, False), zspec(OFF_FB, False), zspec(OFF_I, False), zspec(OFF_G, False, True),
            pl.BlockSpec((1, SUBLANES, LANES), lambda b, h, c: (h, 0, 0)),
            const3(cmat.shape), const3(lvl.shape),
            pl.BlockSpec(emat.shape, lambda b, h, c: (0, 0)),
        ],
        out_specs=pl.BlockSpec((seq, LANES), lambda b, h, c: (b, h)),
        out_shape=jax.ShapeDtypeStruct((m, HG_WIDTH), BF16),
        scratch_shapes=[
            pltpu.VMEM((seq, LANES), F32),
            pltpu.VMEM((HEAD_DIM, HEAD_DIM), F32),
            pltpu.VMEM((HEAD_DIM, HEAD_DIM), F32),
        ],
        compiler_params=_params("parallel", "parallel", "arbitrary"),
        name="hgrn2",
    )(z, z, z, z, z, z, z, z, crow, cmat, lvl, emat)


def _qproj_kernel(z_ref, g_ref, w_ref, cos_ref, sin_ref, o_ref):
    xn = _rms(z_ref[...], g_ref[...]).astype(BF16)
    cz = cos_ref[...]
    sz = sin_ref[...]
    for h in range(HEADS):
        acc = _dot(xn, w_ref[:, h * 3 * LANES:(h + 1) * 3 * LANES])
        o_ref[:, h * Q_PAD:h * Q_PAD + LANES] = (acc[:, :LANES] * QK_SCALE).astype(BF16)
        rope = acc[:, LANES:2 * LANES] * cz + acc[:, 2 * LANES:] * sz
        o_ref[:, h * Q_PAD + LANES:(h + 1) * Q_PAD] = (rope * QK_SCALE).astype(BF16)


def _qproj(z, g, w, cosz, sinz, seq):
    m = z.shape[0]
    tm = _pick(seq, 512, 256, 128)
    ns = seq // tm
    return pl.pallas_call(
        _qproj_kernel,
        grid=(m // tm,),
        in_specs=[
            pl.BlockSpec((tm, LORA), lambda i: (i, OFF_QA // LORA)),
            pl.BlockSpec((1, LORA), lambda i: (0, 0)),
            pl.BlockSpec(w.shape, lambda i: (0, 0)),
            pl.BlockSpec((tm, LANES), lambda i: (i % ns, 0)),
            pl.BlockSpec((tm, LANES), lambda i: (i % ns, 0)),
        ],
        out_specs=pl.BlockSpec((tm, HEADS * Q_PAD), lambda i: (i, 0)),
        out_shape=jax.ShapeDtypeStruct((m, HEADS * Q_PAD), BF16),
        compiler_params=_params("parallel"),
        name="mla_q_proj",
    )(z, g.reshape(1, LORA), w, cosz, sinz)


def _kvproj_kernel(z_ref, kr_ref, g_ref, wk_ref, wvt_ref, cos_ref, sin_ref, k_ref, vt_ref):
    xn = _rms(z_ref[...], g_ref[...]).astype(BF16)
    kr = kr_ref[...]
    krope = (kr[:, :LANES] * cos_ref[...] + kr[:, LANES:] * sin_ref[...]).astype(BF16)
    kn = _dot(xn, wk_ref[...])
    for h in range(HEADS):
        k_ref[:, h * Q_PAD:h * Q_PAD + LANES] = kn[:, h * LANES:(h + 1) * LANES].astype(BF16)
        k_ref[:, h * Q_PAD + LANES:(h + 1) * Q_PAD] = krope
    vt = _dot_nt(wvt_ref[...], xn).astype(BF16)
    ones = jnp.ones((VT_ROWS - HEAD_DIM, vt.shape[1]), BF16)
    for h in range(HEADS):
        vt_ref[0, 0, h, :HEAD_DIM, :] = vt[h * HEAD_DIM:(h + 1) * HEAD_DIM, :]
        vt_ref[0, 0, h, HEAD_DIM:, :] = ones


def _kvproj(z, zkr, g, wk, wvt, cosz, sinz, nb, seq, tk):
    m = z.shape[0]
    ns = seq // tk
    return pl.pallas_call(
        _kvproj_kernel,
        grid=(m // tk,),
        in_specs=[
            pl.BlockSpec((tk, LORA), lambda i: (i, OFF_KVA // LORA)),
            pl.BlockSpec((tk, 2 * LANES), lambda i: (i, 0)),
            pl.BlockSpec((1, LORA), lambda i: (0, 0)),
            pl.BlockSpec(wk.shape, lambda i: (0, 0)),
            pl.BlockSpec(wvt.shape, lambda i: (0, 0)),
            pl.BlockSpec((tk, LANES), lambda i: (i % ns, 0)),
            pl.BlockSpec((tk, LANES), lambda i: (i % ns, 0)),
        ],
        out_specs=[
            pl.BlockSpec((tk, HEADS * Q_PAD), lambda i: (i, 0)),
            pl.BlockSpec((1, 1, HEADS, VT_ROWS, tk), lambda i: (i // ns, i % ns, 0, 0, 0)),
        ],
        out_shape=[
            jax.ShapeDtypeStruct((m, HEADS * Q_PAD), BF16),
            jax.ShapeDtypeStruct((nb, ns, HEADS, VT_ROWS, tk), BF16),
        ],
        compiler_params=_params("parallel"),
        name="mla_kv_proj",
    )(z, zkr, g.reshape(1, LORA), wk, wvt, cosz, sinz)


def _col_reduce(x, op, slab=64):
    r, c = x.shape
    if r > slab and r % slab == 0:
        x = op(x.reshape(r // slab, slab, c), axis=0)
    return op(x, axis=0, keepdims=True)


def _attn_kernel(q_ref, k_ref, vt_ref, o_ref, acc_ref, st_ref, mx_ref, *, tk, nk, nq, nblk):
    tq = acc_ref.shape[1]
    nslot, ng, _, gw = st_ref.shape
    assert nblk % nslot == 0 and nk % nblk == 0
    tpq = nk // nblk

    def scores(qi, j, slot, g):
        kb = k_ref[pl.ds(pl.multiple_of(j * tk, tk), tk), :]
        qg = q_ref[pl.ds(pl.multiple_of(qi * tq + g * gw, gw), gw), :]
        st = _dot_nt(kb, qg)
        st_ref[slot, g] = st
        mx_ref[slot, g] = _col_reduce(st, jnp.max)

    def consume(j, slot, g, m):
        cols = slice(g * gw, (g + 1) * gw)
        m_new = jnp.maximum(m, mx_ref[slot, g])
        alpha = jnp.exp2(m - m_new)
        p = jnp.exp2((st_ref[slot, g] - m_new).astype(BF16))
        acc_ref[:, cols] = alpha * acc_ref[:, cols] + _dot(vt_ref[0, j, 0], p)
        return m_new

    def body(t, ms):
        qi = t // tpq
        j0 = (t % tpq) * nblk
        first = j0 == 0
        last = j0 == nk - nblk

        @pl.when(first)
        def _():
            acc_ref[...] = jnp.zeros_like(acc_ref)

        ms = [jnp.where(first, NEG, m) for m in ms]
        for u in range(nblk):
            if u < nblk - 1:
                nq_i, nj = qi, j0 + u + 1
            else:
                nq_i = jnp.minimum(qi + last.astype(jnp.int32), nq - 1)
                nj = jnp.where(last, 0, j0 + nblk)
            for g in range(ng):
                scores(nq_i, nj, (u + 1) % nslot, g)
                ms[g] = consume(j0 + u, u % nslot, g, ms[g])

        @pl.when(last)
        def _():
            for g in range(ng):
                cols = slice(g * gw, (g + 1) * gw)
                out = acc_ref[:HEAD_DIM, cols] / acc_ref[HEAD_DIM:HEAD_DIM + 1, cols]
                o_ref[pl.ds(pl.multiple_of(qi * tq + g * gw, gw), gw), :] = out.T.astype(BF16)

        return tuple(ms)

    for g in range(ng):
        scores(0, 0, 0, g)
    init = tuple(jnp.full((1, gw), NEG, F32) for _ in range(ng))
    lax.fori_loop(0, nq * tpq, body, init)


def _attention(qp, kp, vt, nb, seq, tk):
    m = qp.shape[0]
    tq = _pick(seq, 1024, 512, 256, 128)
    nq = seq // tq
    nk = seq // tk
    nblk = ATT_BLOCKS if nk % ATT_BLOCKS == 0 else 2
    assert nk % nblk == 0, "the key loop handles an even number of blocks per trip"
    gw = min(ATT_GROUP, tq)
    return pl.pallas_call(
        functools.partial(_attn_kernel, tk=tk, nk=nk, nq=nq, nblk=nblk),
        grid=(nb, HEADS),
        in_specs=[
            pl.BlockSpec((seq, Q_PAD), lambda b, h: (b, h)),
            pl.BlockSpec((seq, Q_PAD), lambda b, h: (b, h)),
            pl.BlockSpec((1, nk, 1, VT_ROWS, tk), lambda b, h: (b, 0, h, 0, 0)),
        ],
        out_specs=pl.BlockSpec((seq, HEAD_DIM), lambda b, h: (b, h)),
        out_shape=jax.ShapeDtypeStruct((m, HG_WIDTH), BF16),
        scratch_shapes=[pltpu.VMEM((VT_ROWS, tq), F32), pltpu.VMEM((2, tq // gw, tk, gw), F32),
                        pltpu.VMEM((2, tq // gw, 1, gw), F32)],
        compiler_params=_params("parallel", "arbitrary"),
        name="mla_attention",
    )(qp, kp, vt)


def _mm_res_kernel(*refs, n_in):
    res_ref, o_ref = refs[2 * n_in], refs[2 * n_in + 1]
    acc = res_ref[...]
    for a_ref, w_ref in zip(refs[:n_in], refs[n_in:2 * n_in]):
        acc = acc + _dot(a_ref[...], w_ref[...])
    o_ref[...] = acc


def _mm_res(a_list, w, res, tm, tn):
    m, n = res.shape
    n_in = len(a_list)
    kb = a_list[0].shape[1]
    assert all(a.shape[1] == kb for a in a_list) and w.shape[0] == n_in * kb
    w_list = [w] * n_in
    in_specs = [pl.BlockSpec((tm, kb), lambda i, j: (i, 0)) for _ in a_list]
    in_specs += [pl.BlockSpec((kb, tn), functools.partial(lambda i, j, r: (r, j), r=r))
                 for r in range(n_in)]
    in_specs += [pl.BlockSpec((tm, tn), lambda i, j: (i, j))]
    return pl.pallas_call(
        functools.partial(_mm_res_kernel, n_in=n_in),
        grid=(m // tm, n // tn),
        in_specs=in_specs,
        out_specs=pl.BlockSpec((tm, tn), lambda i, j: (i, j)),
        out_shape=jax.ShapeDtypeStruct((m, n), F32),
        compiler_params=_params("parallel", "arbitrary"),
        name="mm_residual",
    )(*a_list, *w_list, res)


HALO = 16


def _ffn_up_kernel(x_ref, xp_ref, xn_ref, g_ref, wg_ref, wu_ref, cw_ref, cb_ref, o_ref, n_ref,
                   *, tm, tiles_per_seq):
    i = pl.program_id(0)

    @pl.when(pl.program_id(1) == 0)
    def _():
        g = g_ref[...]
        first = (i % tiles_per_seq) == 0
        last = (i % tiles_per_seq) == tiles_per_seq - 1
        n_ref[0:HALO, :] = jnp.where(first, 0.0, _rms(xp_ref[...], g)).astype(BF16)
        n_ref[HALO:HALO + tm, :] = _rms(x_ref[...], g).astype(BF16)
        n_ref[HALO + tm:, :] = jnp.where(last, 0.0, _rms(xn_ref[...], g)).astype(BF16)

    ge = _dot(n_ref[...], wg_ref[...])
    rows = tm + 2 * HALO
    prev = pltpu.roll(ge, 1, 0)[HALO:HALO + tm, :]
    nxt = pltpu.roll(ge, rows - 1, 0)[HALO:HALO + tm, :]
    cw = cw_ref[...]
    gate = prev * cw[0:1, :] + ge[HALO:HALO + tm, :] * cw[1:2, :] + nxt * cw[2:3, :] + cb_ref[...]
    up = _dot(n_ref[HALO:HALO + tm, :], wu_ref[...])
    o_ref[...] = (gate * _sigmoid(gate) * up).astype(BF16)


def _ffn_up(h, g, w_up, conv_w, conv_b, seq):
    m, k = h.shape
    tm = _pick(seq, 1024, 512, 256, 128)
    tf = 512
    nf = D_FF // tf
    tps = seq // tm
    r = tm // HALO
    nblk = m // HALO
    return pl.pallas_call(
        functools.partial(_ffn_up_kernel, tm=tm, tiles_per_seq=tps),
        grid=(m // tm, nf),
        in_specs=[
            pl.BlockSpec((tm, k), lambda i, j: (i, 0)),
            pl.BlockSpec((HALO, k), lambda i, j: (jnp.maximum(i * r - 1, 0), 0)),
            pl.BlockSpec((HALO, k), lambda i, j: (jnp.minimum((i + 1) * r, nblk - 1), 0)),
            pl.BlockSpec((1, k), lambda i, j: (0, 0)),
            pl.BlockSpec((k, tf), lambda i, j: (0, j)),
            pl.BlockSpec((k, tf), lambda i, j: (0, j + nf)),
            pl.BlockSpec((3, tf), lambda i, j: (0, j)),
            pl.BlockSpec((1, tf), lambda i, j: (0, j)),
        ],
        out_specs=pl.BlockSpec((tm, tf), lambda i, j: (i, j)),
        out_shape=jax.ShapeDtypeStruct((m, D_FF), BF16),
        scratch_shapes=[pltpu.VMEM((tm + 2 * HALO, k), BF16)],
        compiler_params=_params("parallel", "arbitrary"),
        name="ffn_up",
    )(h, h, h, g.reshape(1, k), w_up, w_up, conv_w, conv_b.reshape(1, D_FF))


def _ple_kernel(h_ref, p_ref, g_ref, wg_ref, wp_ref, fn_ref, *out_refs, split, tn):
    x = h_ref[...]
    xn = _rms(x, g_ref[...]).astype(BF16)
    pb = p_ref[...].astype(BF16)
    y_ref = out_refs[-1]
    for c in range(D_MODEL // tn):
        sl = slice(c * tn, (c + 1) * tn)
        gate = _dot(xn, wg_ref[:, sl])
        y_ref[:, sl] = x[:, sl] + _dot(pb, wp_ref[:, sl]) * _sigmoid(gate)
    if split is not None:
        first_ref, second_ref, _ = out_refs
        y = _rms(y_ref[...], fn_ref[...])
        i = pl.program_id(0)

        @pl.when(i < split)
        def _():
            first_ref[...] = y

        @pl.when(i >= split)
        def _():
            second_ref[...] = y


def _ple(h, p, g, wg, wp, fn, m_first=None):
    m, k = h.shape
    tm = _pick(math.gcd(m, m_first or m), 512, 256, 128)
    tile = pl.BlockSpec((tm, k), lambda i: (i, 0))
    if m_first is None:
        split, scratch = None, []
        out_specs, out_shape = tile, jax.ShapeDtypeStruct((m, k), F32)
    else:
        split, scratch = m_first // tm, [pltpu.VMEM((tm, k), F32)]
        out_specs = [pl.BlockSpec((tm, k), lambda i: (jnp.minimum(i, split - 1), 0)),
                     pl.BlockSpec((tm, k), lambda i: (jnp.maximum(i - split, 0), 0))]
        out_shape = [jax.ShapeDtypeStruct((m_first, k), F32), jax.ShapeDtypeStruct((m - m_first, k), F32)]
    return pl.pallas_call(
        functools.partial(_ple_kernel, split=split, tn=512),
        grid=(m // tm,),
        in_specs=[
            tile,
            pl.BlockSpec((tm, PLE_DIM), lambda i: (i, 0)),
            pl.BlockSpec((1, k), lambda i: (0, 0)),
            pl.BlockSpec(wg.shape, lambda i: (0, 0)),
            pl.BlockSpec(wp.shape, lambda i: (0, 0)),
            pl.BlockSpec((1, k), lambda i: (0, 0)),
        ],
        out_specs=out_specs,
        out_shape=out_shape,
        scratch_shapes=scratch,
        compiler_params=_params("arbitrary"),
        name="ple_gate",
    )(h, p, g.reshape(1, k), wg, wp, fn.reshape(1, k))


def _rot_cols(w):
    half = ROPE_DIM // 2
    return jnp.concatenate([-w[..., half:], w[..., :half]], axis=-1)


def _pad_rope(w):
    return jnp.concatenate([w, jnp.zeros_like(w)], axis=-1)


def _prepare(attn_norm, w_in, hg_lower, hg_norm, w_q_b, w_kv_b, w_out, w_up, w_down,
             w_ple_gate, w_ple_proj):
    w_kr = w_in[:, :, OFF_KR:]
    wq = w_q_b.reshape(DEPTH, LORA, HEADS, HEAD_DIM + ROPE_DIM)
    wq_r = wq[..., HEAD_DIM:]
    wq = jnp.concatenate([wq[..., :HEAD_DIM], _pad_rope(wq_r), _pad_rope(_rot_cols(wq_r))], axis=-1)
    wkv = w_kv_b.reshape(DEPTH, LORA, HEADS, 2 * HEAD_DIM)

    lb = jnp.cumsum(jax.nn.softmax(hg_lower.astype(F32), axis=1), axis=1)
    lb = jnp.maximum(lb - lb[:, :1], 0.0)
    loglb = jnp.maximum(jnp.log(lb) * LOG2E, NEG).reshape(2, DEPTH, HEADS, HEAD_DIM)
    log1m = (jnp.log1p(-lb) * LOG2E).reshape(2, DEPTH, HEADS, HEAD_DIM)
    gain = hg_norm.astype(F32).reshape(DEPTH, HEADS, HEAD_DIM)
    zeros = jnp.zeros_like(gain)
    crow = jnp.stack([loglb[0], log1m[0], loglb[1], log1m[1], gain, zeros, zeros, zeros], axis=2)

    return dict(
        w_in=w_in[:, :, :OFF_KR].astype(BF16),
        w_kr=jnp.concatenate([_pad_rope(w_kr), _pad_rope(_rot_cols(w_kr))], axis=-1).astype(BF16),
        w_q=wq.reshape(DEPTH, LORA, HEADS * 3 * LANES).astype(BF16),
        w_k=wkv[..., :HEAD_DIM].reshape(DEPTH, LORA, HG_WIDTH).astype(BF16),
        w_vt=jnp.swapaxes(wkv[..., HEAD_DIM:].reshape(DEPTH, LORA, HG_WIDTH), 1, 2).astype(BF16),
        w_out=w_out.astype(BF16),
        w_up=w_up.astype(BF16),
        w_down=w_down.astype(BF16),
        w_ple_gate=w_ple_gate.astype(BF16),
        w_ple_proj=w_ple_proj.astype(BF16),
        crow=crow,
    )


def _rope_tables(seq):
    inv = ROPE_THETA ** (-jnp.arange(0, ROPE_DIM, 2, dtype=F32) / ROPE_DIM)
    ang = jnp.arange(seq, dtype=F32)[:, None] * inv[None, :]
    pad = jnp.zeros((seq, LANES - ROPE_DIM), F32)
    cosz = jnp.concatenate([jnp.cos(ang), jnp.cos(ang), pad], axis=-1)
    sinz = jnp.concatenate([jnp.sin(ang), jnp.sin(ang), pad], axis=-1)
    return cosz, sinz


def _trunk(x, p, nb_first, attn_norm, q_a_norm, kv_a_norm, ffn_norm, conv_w, conv_b, ple_norm,
           final_norm, prm):
    nb, seq, d = x.shape
    m = nb * seq
    h = x.reshape(m, d)
    p = p.reshape(DEPTH, m, PLE_DIM)
    cosz, sinz = _rope_tables(seq)
    tm = _pick(seq, 1024, 512, 256, 128)
    tk = _pick(seq, 256, 128)
    for l in range(DEPTH):
        z, zkr = _norm_mm(h, attn_norm[l], prm["w_in"][l], prm["w_kr"][l], tm, 1024)
        o_hg = _hgrn(z, prm["crow"][l], nb, seq)
        qp = _qproj(z, q_a_norm[l], prm["w_q"][l], cosz, sinz, seq)
        kp, vt = _kvproj(z, zkr, kv_a_norm[l], prm["w_k"][l], prm["w_vt"][l], cosz, sinz, nb, seq, tk)
        o_mla = _attention(qp, kp, vt, nb, seq, tk)
        h = _mm_res([o_hg, o_mla], prm["w_out"][l], h, tm, 512)
        act = _ffn_up(h, ffn_norm[l], prm["w_up"][l], conv_w[l], conv_b[l], seq)
        h = _mm_res([act], prm["w_down"][l], h, tm, 512)
        h = _ple(h, p[l], ple_norm[l], prm["w_ple_gate"][l], prm["w_ple_proj"][l], final_norm,
                 m_first=nb_first * seq if l == DEPTH - 1 else None)
    return h[0].reshape(nb_first, seq, d), h[1].reshape(nb - nb_first, seq, d)


def kernel(x_prompt, x_sample, p_prompt, p_sample, attn_norm, w_in, hg_lower, hg_norm, q_a_norm,
           w_q_b, kv_a_norm, w_kv_b, w_out, ffn_norm, w_up, conv_w, conv_b, w_down, ple_norm,
           w_ple_gate, w_ple_proj, final_norm):
    assert x_prompt.shape[1:] == x_sample.shape[1:], "both groups must share the sequence length"
    nbp = x_prompt.shape[0]
    prm = _prepare(attn_norm, w_in, hg_lower, hg_norm, w_q_b, w_kv_b, w_out, w_up, w_down,
                   w_ple_gate, w_ple_proj)
    x = jnp.concatenate([x_prompt, x_sample], axis=0)
    p = jnp.concatenate([p_prompt, p_sample], axis=1)
    return _trunk(x, p, nbp, attn_norm, q_a_norm, kv_a_norm, ffn_norm, conv_w, conv_b, ple_norm,
                  final_norm, prm)
```

```python
import functools
import math

import jax
import jax.numpy as jnp
import numpy as np
from jax import lax
from jax.experimental import pallas as pl
from jax.experimental.pallas import tpu as pltpu

F32 = jnp.float32
BF16 = jnp.bfloat16

D_MODEL = 2048
DEPTH = 4
HEADS = 8
HEAD_DIM = 128
ROPE_DIM = 64
LORA = 512
HG_WIDTH = HEADS * HEAD_DIM
OFF_Q, OFF_FF, OFF_FB, OFF_I, OFF_G, OFF_QA, OFF_KVA, OFF_KR = (
    0, 1024, 2048, 3072, 4096, 5120, 5632, 6144)
D_FF = 5632
PLE_DIM = 256
EPS = 1e-6
ROPE_THETA = 10000.0
NEG = -1e30

LANES = 128
SUBLANES = 8
VMEM_LIMIT = 56 * 2**20

HG_CHUNK = 64
HG_ROWS = SUBLANES
HG_STEP = 512
HG_UNROLL = 8
Q_PAD = 2 * LANES
LOG2E = math.log2(math.e)
QK_SCALE = (HEAD_DIM + ROPE_DIM) ** -0.5 * LOG2E
ATT_GROUP = 2 * LANES
ATT_BLOCKS = 2
VT_ROWS = HEAD_DIM + 16


def _params(*sem):
    return pltpu.CompilerParams(dimension_semantics=sem, vmem_limit_bytes=VMEM_LIMIT)


def _pick(n, *cands):
    for c in cands:
        if n % c == 0:
            return c
    raise ValueError(f"no tile for {n} in {cands}")


def _rms(x, g):
    ms = jnp.mean(x * x, axis=-1, keepdims=True)
    return x * lax.rsqrt(ms + EPS) * g


def _sigmoid(x):
    return 1.0 / (1.0 + jnp.exp(-x))


def _dot(a, b):
    return jnp.dot(a, b, preferred_element_type=F32)


def _dot_nt(a, b):
    return lax.dot_general(a, b, (((1,), (1,)), ((), ())), preferred_element_type=F32)


def _dot_tn(a, b):
    return lax.dot_general(a, b, (((0,), (0,)), ((), ())), preferred_element_type=F32)


def _norm_mm_kernel(x_ref, g_ref, w_ref, ws_ref, o_ref, os_ref, xn_ref):
    @pl.when(pl.program_id(1) == 0)
    def _():
        xn_ref[...] = _rms(x_ref[...], g_ref[...]).astype(BF16)
        os_ref[...] = _dot(xn_ref[...], ws_ref[...])

    o_ref[...] = _dot(xn_ref[...], w_ref[...])


def _norm_mm(x, g, w, w_side, tm, tn):
    m, k = x.shape
    n = w.shape[1]
    ns = w_side.shape[1]
    return pl.pallas_call(
        _norm_mm_kernel,
        grid=(m // tm, n // tn),
        in_specs=[
            pl.BlockSpec((tm, k), lambda i, j: (i, 0)),
            pl.BlockSpec((1, k), lambda i, j: (0, 0)),
            pl.BlockSpec((k, tn), lambda i, j: (0, j)),
            pl.BlockSpec((k, ns), lambda i, j: (0, 0)),
        ],
        out_specs=[pl.BlockSpec((tm, tn), lambda i, j: (i, j)),
                   pl.BlockSpec((tm, ns), lambda i, j: (i, 0))],
        out_shape=[jax.ShapeDtypeStruct((m, n), F32), jax.ShapeDtypeStruct((m, ns), F32)],
        scratch_shapes=[pltpu.VMEM((tm, k), BF16)],
        compiler_params=_params("parallel", "arbitrary"),
        name="norm_mm",
    )(x, g.reshape(1, k), w, w_side)


def _hgrn_gates(z, loglb, log1m):
    z2 = z * LOG2E
    l1p = jnp.log2(1.0 + jnp.exp2(-jnp.abs(z2)))
    c = log1m + (jnp.minimum(z2, 0.0) - l1p)
    logf = jnp.maximum(loglb, c) + jnp.log2(1.0 + jnp.exp2(-jnp.abs(loglb - c)))
    return logf, c - z2


def _hgrn_chunk(q, zf, v, cmat, lvl, emat, loglb, log1m, st_ref, reverse):
    c = q.shape[0]
    nv = c // HG_ROWS
    logf, logk = _hgrn_gates(zf, loglb, log1m)

    hi = logf.astype(BF16)
    lo = (logf - hi.astype(F32)).astype(BF16)
    yield
    bb = _dot(cmat, jnp.concatenate([hi, lo], axis=1))
    b = bb[:, :LANES] + bb[:, LANES:]

    def rows(x, p):
        i = nv - 1 - p if reverse else p
        return x[i * HG_ROWS:(i + 1) * HG_ROWS, :]

    def assemble(groups):
        return jnp.concatenate(groups[::-1] if reverse else groups, axis=0)

    edge = 0 if reverse else HG_ROWS - 1
    b_p = [rows(b, p) for p in range(nv)]
    q_p = [rows(q, p) for p in range(nv)]
    lk_p = [rows(logk, p) for p in range(nv)]
    bnd = [jnp.broadcast_to(x[edge:edge + 1, :], (HG_ROWS, LANES)) for x in b_p]
    zero = jnp.zeros((HG_ROWS, LANES), F32)

    a = jnp.zeros((c, c), F32)
    level, mv = 1, nv // 2
    while mv >= 1:
        qm = [zero] * nv
        km = [zero] * nv
        for g in range(nv // (2 * mv)):
            ref = bnd[2 * mv * g + mv - 1]
            for p in range(2 * mv * g, 2 * mv * g + mv):
                km[p] = jnp.exp2(lk_p[p] + (ref - b_p[p]))
            for p in range(2 * mv * g + mv, 2 * mv * (g + 1)):
                qm[p] = q_p[p] * jnp.exp2(b_p[p] - ref)
        yield
        al = _dot_nt(assemble(qm).astype(BF16), assemble(km).astype(BF16))
        a = jnp.where(lvl == float(level), al, a)
        level += 1
        mv //= 2

    rib = lax.broadcasted_iota(jnp.int32, (HG_ROWS, LANES), 0)
    slabs = []
    for s in range(HG_ROWS):
        valid = (rib <= s) if reverse else (rib >= s)
        col = []
        for p in range(nv):
            cs = jnp.broadcast_to((b_p[p] - lk_p[p])[s:s + 1, :], (HG_ROWS, LANES))
            col.append(q_p[p] * jnp.exp2(jnp.where(valid, b_p[p] - cs, NEG)))
        slabs.append(assemble(col).astype(BF16))
        yield
    adiag = _dot(jnp.concatenate(slabs, axis=1), emat)
    a = jnp.where(lvl == float(level), adiag, a)

    tot = bnd[nv - 1]
    qbar = assemble([q_p[p] * jnp.exp2(b_p[p]) for p in range(nv)]).astype(BF16)
    kbar = assemble([jnp.exp2(lk_p[p] + (tot - b_p[p])) for p in range(nv)]).astype(BF16)
    vb = v.astype(BF16)
    yield
    st = st_ref[...]
    o = _dot(a.astype(BF16), vb) + _dot_nt(qbar, st.astype(BF16))
    st_ref[...] = jnp.exp2(tot[0:1, :]) * st + _dot_tn(vb, kbar)
    return o


def _lockstep(gens):
    results = [None] * len(gens)
    live = list(range(len(gens)))
    while live:
        for idx in list(live):
            try:
                next(gens[idx])
            except StopIteration as done:
                results[idx] = done.value
                live.remove(idx)
    return results


def _hgrn_kernel(qf_ref, zf_ref, vf_ref, gf_ref, qb_ref, zb_ref, vb_ref, gb_ref,
                 crow_ref, cmat_ref, lvl_ref, emat_ref, o_ref,
                 part_ref, stf_ref, stb_ref, *, step, chunk, nsteps):
    c = pl.program_id(2)

    @pl.when(c == 0)
    def _():
        stf_ref[...] = jnp.zeros_like(stf_ref)
        stb_ref[...] = jnp.zeros_like(stb_ref)
        part_ref[...] = jnp.zeros_like(part_ref)

    crow = crow_ref[0]
    gain = crow[4:5, :]
    emat = emat_ref[...]
    nch = step // chunk

    def emit(o, row, g):
        t = o + part_ref[pl.ds(row, chunk), :]
        part_ref[pl.ds(row, chunk), :] = t
        o_ref[pl.ds(row, chunk), :] = (_rms(t, gain) * (g * _sigmoid(g))).astype(BF16)

    def fwd(j):
        rf = pl.multiple_of(j * chunk, chunk)
        sl = pl.ds(rf, chunk)
        gen = _hgrn_chunk(qf_ref[sl, :], zf_ref[sl, :], vf_ref[sl, :], cmat_ref[0], lvl_ref[0], emat,
                          crow[0:1, :], crow[1:2, :], stf_ref, False)
        return gen, pl.multiple_of(c * step + rf, chunk), gf_ref, sl

    def bwd(j):
        rb = pl.multiple_of((nch - 1 - j) * chunk, chunk)
        sl = pl.ds(rb, chunk)
        gen = _hgrn_chunk(qb_ref[sl, :], zb_ref[sl, :], vb_ref[sl, :], cmat_ref[1], lvl_ref[1], emat,
                          crow[2:3, :], crow[3:4, :], stb_ref, True)
        return gen, pl.multiple_of((nsteps - 1 - c) * step + rb, chunk), gb_ref, sl

    unroll = math.gcd(HG_UNROLL, nch)

    def body(i, carry):
        chains = [f(i * unroll + u) for u in range(unroll) for f in (fwd, bwd)]
        outs = _lockstep([ch[0] for ch in chains])
        for o, (_, row, g_ref, sl) in zip(outs, chains):
            emit(o, row, g_ref[sl, :])
        return carry

    lax.fori_loop(0, nch // unroll, body, 0)


def _hgrn_consts(chunk):
    t = np.arange(chunk)[:, None]
    s = np.arange(chunk)[None, :]
    cm = np.stack([(s <= t), (s >= t)]).astype(np.float32)
    nlev = int(round(math.log2(chunk // HG_ROWS)))
    lv = np.zeros((chunk, chunk), np.float32)
    for level in range(1, nlev + 1):
        m = chunk >> level
        hit = (t // (2 * m) == s // (2 * m)) & (t // m != s // m) & (s < t)
        lv[hit] = level
    lv[(t // HG_ROWS) == (s // HG_ROWS)] = nlev + 1
    off = (lv >= 1) & (lv <= nlev)
    lvb = np.where(off.T, lv.T, 0.0)
    lvb[(t // HG_ROWS) == (s // HG_ROWS)] = nlev + 1
    lvl = np.stack([lv, lvb]).astype(np.float32)
    em = (np.arange(HG_ROWS * LANES)[:, None] // LANES == (np.arange(chunk)[None, :] % HG_ROWS))
    return jnp.asarray(cm, BF16), jnp.asarray(lvl, F32), jnp.asarray(em.astype(np.float32), BF16)


def _hgrn(z, crow, nb, seq):
    m = z.shape[0]
    step = _pick(seq // 2, HG_STEP, 256, 128, 64)
    chunk = HG_CHUNK
    n = seq // step
    half = n // 2
    cmat, lvl, emat = _hgrn_consts(chunk)

    def zspec(col, fwd, gate=False):
        if fwd:
            blk = (lambda c: jnp.maximum(c, half)) if gate else (lambda c: c)
        else:
            blk = (lambda c: jnp.minimum(n - 1 - c, half - 1)) if gate else (lambda c: n - 1 - c)
        return pl.BlockSpec((step, LANES), lambda b, h, c: (b * n + blk(c), col // LANES + h))

    const3 = lambda shape: pl.BlockSpec(shape, lambda b, h, c: (0, 0, 0))
    return pl.pallas_call(
        functools.partial(_hgrn_kernel, step=step, chunk=chunk, nsteps=n),
        grid=(nb, HEADS, n),
        in_specs=[
            zspec(OFF_Q, True), zspec(OFF_FF, True), zspec(OFF_I, True), zspec(OFF_G, True, True),
            zspec(OFF_Q, False), zspec(OFF_FB, False), zspec(OFF_I, False), zspec(OFF_G, False, True),
            pl.BlockSpec((1, SUBLANES, LANES), lambda b, h, c: (h, 0, 0)),
            const3(cmat.shape), const3(lvl.shape),
            pl.BlockSpec(emat.shape, lambda b, h, c: (0, 0)),
        ],
        out_specs=pl.BlockSpec((seq, LANES), lambda b, h, c: (b, h)),
        out_shape=jax.ShapeDtypeStruct((m, HG_WIDTH), BF16),
        scratch_shapes=[
            pltpu.VMEM((seq, LANES), F32),
            pltpu.VMEM((HEAD_DIM, HEAD_DIM), F32),
            pltpu.VMEM((HEAD_DIM, HEAD_DIM), F32),
        ],
        compiler_params=_params("parallel", "parallel", "arbitrary"),
        name="hgrn2",
    )(z, z, z, z, z, z, z, z, crow, cmat, lvl, emat)


def _qproj_kernel(z_ref, g_ref, w_ref, cos_ref, sin_ref, o_ref):
    xn = _rms(z_ref[...], g_ref[...]).astype(BF16)
    cz = cos_ref[...]
    sz = sin_ref[...]
    for h in range(HEADS):
        acc = _dot(xn, w_ref[:, h * 3 * LANES:(h + 1) * 3 * LANES])
        o_ref[:, h * Q_PAD:h * Q_PAD + LANES] = (acc[:, :LANES] * QK_SCALE).astype(BF16)
        rope = acc[:, LANES:2 * LANES] * cz + acc[:, 2 * LANES:] * sz
        o_ref[:, h * Q_PAD + LANES:(h + 1) * Q_PAD] = (rope * QK_SCALE).astype(BF16)


def _qproj(z, g, w, cosz, sinz, seq):
    m = z.shape[0]
    tm = _pick(seq, 512, 256, 128)
    ns = seq // tm
    return pl.pallas_call(
        _qproj_kernel,
        grid=(m // tm,),
        in_specs=[
            pl.BlockSpec((tm, LORA), lambda i: (i, OFF_QA // LORA)),
            pl.BlockSpec((1, LORA), lambda i: (0, 0)),
            pl.BlockSpec(w.shape, lambda i: (0, 0)),
            pl.BlockSpec((tm, LANES), lambda i: (i % ns, 0)),
            pl.BlockSpec((tm, LANES), lambda i: (i % ns, 0)),
        ],
        out_specs=pl.BlockSpec((tm, HEADS * Q_PAD), lambda i: (i, 0)),
        out_shape=jax.ShapeDtypeStruct((m, HEADS * Q_PAD), BF16),
        compiler_params=_params("parallel"),
        name="mla_q_proj",
    )(z, g.reshape(1, LORA), w, cosz, sinz)


def _kvproj_kernel(z_ref, kr_ref, g_ref, wk_ref, wvt_ref, cos_ref, sin_ref, k_ref, vt_ref):
    xn = _rms(z_ref[...], g_ref[...]).astype(BF16)
    kr = kr_ref[...]
    krope = (kr[:, :LANES] * cos_ref[...] + kr[:, LANES:] * sin_ref[...]).astype(BF16)
    kn = _dot(xn, wk_ref[...])
    for h in range(HEADS):
        k_ref[:, h * Q_PAD:h * Q_PAD + LANES] = kn[:, h * LANES:(h + 1) * LANES].astype(BF16)
        k_ref[:, h * Q_PAD + LANES:(h + 1) * Q_PAD] = krope
    vt = _dot_nt(wvt_ref[...], xn).astype(BF16)
    ones = jnp.ones((VT_ROWS - HEAD_DIM, vt.shape[1]), BF16)
    for h in range(HEADS):
        vt_ref[0, 0, h, :HEAD_DIM, :] = vt[h * HEAD_DIM:(h + 1) * HEAD_DIM, :]
        vt_ref[0, 0, h, HEAD_DIM:, :] = ones


def _kvproj(z, zkr, g, wk, wvt, cosz, sinz, nb, seq, tk):
    m = z.shape[0]
    ns = seq // tk
    return pl.pallas_call(
        _kvproj_kernel,
        grid=(m // tk,),
        in_specs=[
            pl.BlockSpec((tk, LORA), lambda i: (i, OFF_KVA // LORA)),
            pl.BlockSpec((tk, 2 * LANES), lambda i: (i, 0)),
            pl.BlockSpec((1, LORA), lambda i: (0, 0)),
            pl.BlockSpec(wk.shape, lambda i: (0, 0)),
            pl.BlockSpec(wvt.shape, lambda i: (0, 0)),
            pl.BlockSpec((tk, LANES), lambda i: (i % ns, 0)),
            pl.BlockSpec((tk, LANES), lambda i: (i % ns, 0)),
        ],
        out_specs=[
            pl.BlockSpec((tk, HEADS * Q_PAD), lambda i: (i, 0)),
            pl.BlockSpec((1, 1, HEADS, VT_ROWS, tk), lambda i: (i // ns, i % ns, 0, 0, 0)),
        ],
        out_shape=[
            jax.ShapeDtypeStruct((m, HEADS * Q_PAD), BF16),
            jax.ShapeDtypeStruct((nb, ns, HEADS, VT_ROWS, tk), BF16),
        ],
        compiler_params=_params("parallel"),
        name="mla_kv_proj",
    )(z, zkr, g.reshape(1, LORA), wk, wvt, cosz, sinz)


def _col_reduce(x, op, slab=64):
    r, c = x.shape
    if r > slab and r % slab == 0:
        x = op(x.reshape(r // slab, slab, c), axis=0)
    return op(x, axis=0, keepdims=True)


def _attn_kernel(q_ref, k_ref, vt_ref, o_ref, acc_ref, st_ref, mx_ref, *, tk, nk, nblk):
    tq = q_ref.shape[0]
    nslot, ng, _, gw = st_ref.shape
    assert nblk % nslot == 0 and nk % nblk == 0
    acc_ref[...] = jnp.zeros_like(acc_ref)

    def scores(j, slot, g):
        kb = k_ref[pl.ds(pl.multiple_of(j * tk, tk), tk), :]
        st = _dot_nt(kb, q_ref[g * gw:(g + 1) * gw, :])
        st_ref[slot, g] = st
        mx_ref[slot, g] = _col_reduce(st, jnp.max)

    def consume(j, slot, g, m):
        cols = slice(g * gw, (g + 1) * gw)
        m_new = jnp.maximum(m, mx_ref[slot, g])
        alpha = jnp.exp2(m - m_new)
        p = jnp.exp2((st_ref[slot, g] - m_new).astype(BF16))
        acc_ref[:, cols] = alpha * acc_ref[:, cols] + _dot(vt_ref[0, j, 0], p)
        return m_new

    def body(i, ms):
        j = nblk * i
        ms = list(ms)
        for u in range(nblk):
            nxt = j + u + 1
            if u == nblk - 1:
                nxt = jnp.minimum(nxt, nk - 1)
            for g in range(ng):
                scores(nxt, (u + 1) % nslot, g)
                ms[g] = consume(j + u, u % nslot, g, ms[g])
        return tuple(ms)

    for g in range(ng):
        scores(0, 0, g)
    init = tuple(jnp.full((1, gw), NEG, F32) for _ in range(ng))
    lax.fori_loop(0, nk // nblk, body, init)
    for g in range(ng):
        cols = slice(g * gw, (g + 1) * gw)
        o_ref[cols, :] = (acc_ref[:HEAD_DIM, cols] / acc_ref[HEAD_DIM:HEAD_DIM + 1, cols]).T.astype(BF16)


def _attention(qp, kp, vt, nb, seq, tk):
    m = qp.shape[0]
    tq = _pick(seq, 1024, 512, 256, 128)
    nq = seq // tq
    nk = seq // tk
    nblk = ATT_BLOCKS if nk % ATT_BLOCKS == 0 else 2
    assert nk % nblk == 0, "the key loop handles an even number of blocks per trip"
    gw = min(ATT_GROUP, tq)
    return pl.pallas_call(
        functools.partial(_attn_kernel, tk=tk, nk=nk, nblk=nblk),
        grid=(nb, HEADS, nq),
        in_specs=[
            pl.BlockSpec((tq, Q_PAD), lambda b, h, i: (b * nq + i, h)),
            pl.BlockSpec((seq, Q_PAD), lambda b, h, i: (b, h)),
            pl.BlockSpec((1, nk, 1, VT_ROWS, tk), lambda b, h, i: (b, 0, h, 0, 0)),
        ],
        out_specs=pl.BlockSpec((tq, HEAD_DIM), lambda b, h, i: (b * nq + i, h)),
        out_shape=jax.ShapeDtypeStruct((m, HG_WIDTH), BF16),
        scratch_shapes=[pltpu.VMEM((VT_ROWS, tq), F32), pltpu.VMEM((2, tq // gw, tk, gw), F32),
                        pltpu.VMEM((2, tq // gw, 1, gw), F32)],
        compiler_params=_params("parallel", "parallel", "arbitrary"),
        name="mla_attention",
    )(qp, kp, vt)


def _mm_res_kernel(*refs, n_in):
    res_ref, o_ref = refs[2 * n_in], refs[2 * n_in + 1]
    acc = res_ref[...]
    for a_ref, w_ref in zip(refs[:n_in], refs[n_in:2 * n_in]):
        acc = acc + _dot(a_ref[...], w_ref[...])
    o_ref[...] = acc


def _mm_res(a_list, w, res, tm, tn):
    m, n = res.shape
    n_in = len(a_list)
    kb = a_list[0].shape[1]
    assert all(a.shape[1] == kb for a in a_list) and w.shape[0] == n_in * kb
    w_list = [w] * n_in
    in_specs = [pl.BlockSpec((tm, kb), lambda i, j: (i, 0)) for _ in a_list]
    in_specs += [pl.BlockSpec((kb, tn), functools.partial(lambda i, j, r: (r, j), r=r))
                 for r in range(n_in)]
    in_specs += [pl.BlockSpec((tm, tn), lambda i, j: (i, j))]
    return pl.pallas_call(
        functools.partial(_mm_res_kernel, n_in=n_in),
        grid=(m // tm, n // tn),
        in_specs=in_specs,
        out_specs=pl.BlockSpec((tm, tn), lambda i, j: (i, j)),
        out_shape=jax.ShapeDtypeStruct((m, n), F32),
        compiler_params=_params("parallel", "arbitrary"),
        name="mm_residual",
    )(*a_list, *w_list, res)


HALO = 16


def _ffn_up_kernel(x_ref, xp_ref, xn_ref, g_ref, wg_ref, wu_ref, cw_ref, cb_ref, o_ref, n_ref,
                   *, tm, tiles_per_seq):
    i = pl.program_id(0)

    @pl.when(pl.program_id(1) == 0)
    def _():
        g = g_ref[...]
        first = (i % tiles_per_seq) == 0
        last = (i % tiles_per_seq) == tiles_per_seq - 1
        n_ref[0:HALO, :] = jnp.where(first, 0.0, _rms(xp_ref[...], g)).astype(BF16)
        n_ref[HALO:HALO + tm, :] = _rms(x_ref[...], g).astype(BF16)
        n_ref[HALO + tm:, :] = jnp.where(last, 0.0, _rms(xn_ref[...], g)).astype(BF16)

    ge = _dot(n_ref[...], wg_ref[...])
    rows = tm + 2 * HALO
    prev = pltpu.roll(ge, 1, 0)[HALO:HALO + tm, :]
    nxt = pltpu.roll(ge, rows - 1, 0)[HALO:HALO + tm, :]
    cw = cw_ref[...]
    gate = prev * cw[0:1, :] + ge[HALO:HALO + tm, :] * cw[1:2, :] + nxt * cw[2:3, :] + cb_ref[...]
    up = _dot(n_ref[HALO:HALO + tm, :], wu_ref[...])
    o_ref[...] = (gate * _sigmoid(gate) * up).astype(BF16)


def _ffn_up(h, g, w_up, conv_w, conv_b, seq):
    m, k = h.shape
    tm = _pick(seq, 1024, 512, 256, 128)
    tf = 512
    nf = D_FF // tf
    tps = seq // tm
    r = tm // HALO
    nblk = m // HALO
    return pl.pallas_call(
        functools.partial(_ffn_up_kernel, tm=tm, tiles_per_seq=tps),
        grid=(m // tm, nf),
        in_specs=[
            pl.BlockSpec((tm, k), lambda i, j: (i, 0)),
            pl.BlockSpec((HALO, k), lambda i, j: (jnp.maximum(i * r - 1, 0), 0)),
            pl.BlockSpec((HALO, k), lambda i, j: (jnp.minimum((i + 1) * r, nblk - 1), 0)),
            pl.BlockSpec((1, k), lambda i, j: (0, 0)),
            pl.BlockSpec((k, tf), lambda i, j: (0, j)),
            pl.BlockSpec((k, tf), lambda i, j: (0, j + nf)),
            pl.BlockSpec((3, tf), lambda i, j: (0, j)),
            pl.BlockSpec((1, tf), lambda i, j: (0, j)),
        ],
        out_specs=pl.BlockSpec((tm, tf), lambda i, j: (i, j)),
        out_shape=jax.ShapeDtypeStruct((m, D_FF), BF16),
        scratch_shapes=[pltpu.VMEM((tm + 2 * HALO, k), BF16)],
        compiler_params=_params("parallel", "arbitrary"),
        name="ffn_up",
    )(h, h, h, g.reshape(1, k), w_up, w_up, conv_w, conv_b.reshape(1, D_FF))


def _ple_kernel(h_ref, p_ref, g_ref, wg_ref, wp_ref, fn_ref, *out_refs, split, tn):
    x = h_ref[...]
    xn = _rms(x, g_ref[...]).astype(BF16)
    pb = p_ref[...].astype(BF16)
    y_ref = out_refs[-1]
    for c in range(D_MODEL // tn):
        sl = slice(c * tn, (c + 1) * tn)
        gate = _dot(xn, wg_ref[:, sl])
        y_ref[:, sl] = x[:, sl] + _dot(pb, wp_ref[:, sl]) * _sigmoid(gate)
    if split is not None:
        first_ref, second_ref, _ = out_refs
        y = _rms(y_ref[...], fn_ref[...])
        i = pl.program_id(0)

        @pl.when(i < split)
        def _():
            first_ref[...] = y

        @pl.when(i >= split)
        def _():
            second_ref[...] = y


def _ple(h, p, g, wg, wp, fn, m_first=None):
    m, k = h.shape
    tm = _pick(math.gcd(m, m_first or m), 512, 256, 128)
    tile = pl.BlockSpec((tm, k), lambda i: (i, 0))
    if m_first is None:
        split, scratch = None, []
        out_specs, out_shape = tile, jax.ShapeDtypeStruct((m, k), F32)
    else:
        split, scratch = m_first // tm, [pltpu.VMEM((tm, k), F32)]
        out_specs = [pl.BlockSpec((tm, k), lambda i: (jnp.minimum(i, split - 1), 0)),
                     pl.BlockSpec((tm, k), lambda i: (jnp.maximum(i - split, 0), 0))]
        out_shape = [jax.ShapeDtypeStruct((m_first, k), F32), jax.ShapeDtypeStruct((m - m_first, k), F32)]
    return pl.pallas_call(
        functools.partial(_ple_kernel, split=split, tn=512),
        grid=(m // tm,),
        in_specs=[
            tile,
            pl.BlockSpec((tm, PLE_DIM), lambda i: (i, 0)),
            pl.BlockSpec((1, k), lambda i: (0, 0)),
            pl.BlockSpec(wg.shape, lambda i: (0, 0)),
            pl.BlockSpec(wp.shape, lambda i: (0, 0)),
            pl.BlockSpec((1, k), lambda i: (0, 0)),
        ],
        out_specs=out_specs,
        out_shape=out_shape,
        scratch_shapes=scratch,
        compiler_params=_params("arbitrary"),
        name="ple_gate",
    )(h, p, g.reshape(1, k), wg, wp, fn.reshape(1, k))


def _rot_cols(w):
    half = ROPE_DIM // 2
    return jnp.concatenate([-w[..., half:], w[..., :half]], axis=-1)


def _pad_rope(w):
    return jnp.concatenate([w, jnp.zeros_like(w)], axis=-1)


def _prepare(attn_norm, w_in, hg_lower, hg_norm, w_q_b, w_kv_b, w_out, w_up, w_down,
             w_ple_gate, w_ple_proj):
    w_kr = w_in[:, :, OFF_KR:]
    wq = w_q_b.reshape(DEPTH, LORA, HEADS, HEAD_DIM + ROPE_DIM)
    wq_r = wq[..., HEAD_DIM:]
    wq = jnp.concatenate([wq[..., :HEAD_DIM], _pad_rope(wq_r), _pad_rope(_rot_cols(wq_r))], axis=-1)
    wkv = w_kv_b.reshape(DEPTH, LORA, HEADS, 2 * HEAD_DIM)

    lb = jnp.cumsum(jax.nn.softmax(hg_lower.astype(F32), axis=1), axis=1)
    lb = jnp.maximum(lb - lb[:, :1], 0.0)
    loglb = jnp.maximum(jnp.log(lb) * LOG2E, NEG).reshape(2, DEPTH, HEADS, HEAD_DIM)
    log1m = (jnp.log1p(-lb) * LOG2E).reshape(2, DEPTH, HEADS, HEAD_DIM)
    gain = hg_norm.astype(F32).reshape(DEPTH, HEADS, HEAD_DIM)
    zeros = jnp.zeros_like(gain)
    crow = jnp.stack([loglb[0], log1m[0], loglb[1], log1m[1], gain, zeros, zeros, zeros], axis=2)

    return dict(
        w_in=w_in[:, :, :OFF_KR].astype(BF16),
        w_kr=jnp.concatenate([_pad_rope(w_kr), _pad_rope(_rot_cols(w_kr))], axis=-1).astype(BF16),
        w_q=wq.reshape(DEPTH, LORA, HEADS * 3 * LANES).astype(BF16),
        w_k=wkv[..., :HEAD_DIM].reshape(DEPTH, LORA, HG_WIDTH).astype(BF16),
        w_vt=jnp.swapaxes(wkv[..., HEAD_DIM:].reshape(DEPTH, LORA, HG_WIDTH), 1, 2).astype(BF16),
        w_out=w_out.astype(BF16),
        w_up=w_up.astype(BF16),
        w_down=w_down.astype(BF16),
        w_ple_gate=w_ple_gate.astype(BF16),
        w_ple_proj=w_ple_proj.astype(BF16),
        crow=crow,
    )


def _rope_tables(seq):
    inv = ROPE_THETA ** (-jnp.arange(0, ROPE_DIM, 2, dtype=F32) / ROPE_DIM)
    ang = jnp.arange(seq, dtype=F32)[:, None] * inv[None, :]
    pad = jnp.zeros((seq, LANES - ROPE_DIM), F32)
    cosz = jnp.concatenate([jnp.cos(ang), jnp.cos(ang), pad], axis=-1)
    sinz = jnp.concatenate([jnp.sin(ang), jnp.sin(ang), pad], axis=-1)
    return cosz, sinz


def _trunk(x, p, nb_first, attn_norm, q_a_norm, kv_a_norm, ffn_norm, conv_w, conv_b, ple_norm,
           final_norm, prm):
    nb, seq, d = x.shape
    m = nb * seq
    h = x.reshape(m, d)
    p = p.reshape(DEPTH, m, PLE_DIM)
    cosz, sinz = _rope_tables(seq)
    tm = _pick(seq, 1024, 512, 256, 128)
    tk = _pick(seq, 1024, 512, 256, 128)
    for l in range(DEPTH):
        z, zkr = _norm_mm(h, attn_norm[l], prm["w_in"][l], prm["w_kr"][l], tm, 1024)
        o_hg = _hgrn(z, prm["crow"][l], nb, seq)
        qp = _qproj(z, q_a_norm[l], prm["w_q"][l], cosz, sinz, seq)
        kp, vt = _kvproj(z, zkr, kv_a_norm[l], prm["w_k"][l], prm["w_vt"][l], cosz, sinz, nb, seq, tk)
        o_mla = _attention(qp, kp, vt, nb, seq, tk)
        h = _mm_res([o_hg, o_mla], prm["w_out"][l], h, tm, 512)
        act = _ffn_up(h, ffn_norm[l], prm["w_up"][l], conv_w[l], conv_b[l], seq)
        h = _mm_res([act], prm["w_down"][l], h, tm, 512)
        h = _ple(h, p[l], ple_norm[l], prm["w_ple_gate"][l], prm["w_ple_proj"][l], final_norm,
                 m_first=nb_first * seq if l == DEPTH - 1 else None)
    return h[0].reshape(nb_first, seq, d), h[1].reshape(nb - nb_first, seq, d)


def kernel(x_prompt, x_sample, p_prompt, p_sample, attn_norm, w_in, hg_lower, hg_norm, q_a_norm,
           w_q_b, kv_a_norm, w_kv_b, w_out, ffn_norm, w_up, conv_w, conv_b, w_down, ple_norm,
           w_ple_gate, w_ple_proj, final_norm):
    assert x_prompt.shape[1:] == x_sample.shape[1:], "both groups must share the sequence length"
    nbp = x_prompt.shape[0]
    prm = _prepare(attn_norm, w_in, hg_lower, hg_norm, w_q_b, w_kv_b, w_out, w_up, w_down,
                   w_ple_gate, w_ple_proj)
    x = jnp.concatenate([x_prompt, x_sample], axis=0)
    p = jnp.concatenate([p_prompt, p_sample], axis=1)
    return _trunk(x, p, nbp, attn_norm, q_a_norm, kv_a_norm, ffn_norm, conv_w, conv_b, ple_norm,
                  final_norm, prm)
```

```python
import functools
import math

import jax
import jax.numpy as jnp
import numpy as np
from jax import lax
from jax.experimental import pallas as pl
from jax.experimental.pallas import tpu as pltpu

F32 = jnp.float32
BF16 = jnp.bfloat16

D_MODEL = 2048
DEPTH = 4
HEADS = 8
HEAD_DIM = 128
ROPE_DIM = 64
LORA = 512
HG_WIDTH = HEADS * HEAD_DIM
OFF_Q, OFF_FF, OFF_FB, OFF_I, OFF_G, OFF_QA, OFF_KVA, OFF_KR = (
    0, 1024, 2048, 3072, 4096, 5120, 5632, 6144)
D_FF = 5632
PLE_DIM = 256
EPS = 1e-6
ROPE_THETA = 10000.0
NEG = -1e30

LANES = 128
SUBLANES = 8
VMEM_LIMIT = 56 * 2**20

HG_CHUNK = 64
HG_ROWS = SUBLANES
HG_STEP = 512
HG_UNROLL = 8
Q_PAD = 2 * LANES
LOG2E = math.log2(math.e)
QK_SCALE = (HEAD_DIM + ROPE_DIM) ** -0.5 * LOG2E
ATT_GROUP = 2 * LANES
ATT_BLOCKS = 2
VT_ROWS = HEAD_DIM + 16


def _params(*sem):
    return pltpu.CompilerParams(dimension_semantics=sem, vmem_limit_bytes=VMEM_LIMIT)


def _pick(n, *cands):
    for c in cands:
        if n % c == 0:
            return c
    raise ValueError(f"no tile for {n} in {cands}")


def _rms(x, g):
    ms = jnp.mean(x * x, axis=-1, keepdims=True)
    return x * lax.rsqrt(ms + EPS) * g


def _sigmoid(x):
    return 1.0 / (1.0 + jnp.exp(-x))


def _dot(a, b):
    return jnp.dot(a, b, preferred_element_type=F32)


def _dot_nt(a, b):
    return lax.dot_general(a, b, (((1,), (1,)), ((), ())), preferred_element_type=F32)


def _dot_tn(a, b):
    return lax.dot_general(a, b, (((0,), (0,)), ((), ())), preferred_element_type=F32)


def _part_tiles(parts, tm):
    assert all(p.shape[0] % tm == 0 for p in parts)
    return tuple(p.shape[0] // tm for p in parts)


def _part_specs(parts, tm, width, col):
    specs, start = [], 0
    for nt in _part_tiles(parts, tm):
        def imap(i, *rest, start=start, nt=nt):
            inside = (i >= start) & (i < start + nt)
            return jnp.clip(i - start, 0, nt - 1), jnp.where(inside, col(i, *rest), 0)
        specs.append(pl.BlockSpec((tm, width), imap))
        start += nt
    return specs


def _part_read(refs, tiles, i):
    x, start = refs[0][...], tiles[0]
    for ref, nt in zip(refs[1:], tiles[1:]):
        x = jnp.where(i >= start, ref[...], x)
        start += nt
    return x


def _norm_mm_kernel(*refs, tiles):
    x_refs = refs[:len(tiles)]
    g_ref, w_ref, ws_ref, o_ref, os_ref, xn_ref = refs[len(tiles):]

    @pl.when(pl.program_id(1) == 0)
    def _():
        x = _part_read(x_refs, tiles, pl.program_id(0))
        xn_ref[...] = _rms(x, g_ref[...]).astype(BF16)
        os_ref[...] = _dot(xn_ref[...], ws_ref[...])

    o_ref[...] = _dot(xn_ref[...], w_ref[...])


def _norm_mm(x_parts, g, w, w_side, tm, tn):
    k, n = w.shape
    ns = w_side.shape[1]
    tiles = _part_tiles(x_parts, tm)
    m = sum(tiles) * tm
    return pl.pallas_call(
        functools.partial(_norm_mm_kernel, tiles=tiles),
        grid=(m // tm, n // tn),
        in_specs=_part_specs(x_parts, tm, k, lambda i, j: 0) + [
            pl.BlockSpec((1, k), lambda i, j: (0, 0)),
            pl.BlockSpec((k, tn), lambda i, j: (0, j)),
            pl.BlockSpec((k, ns), lambda i, j: (0, 0)),
        ],
        out_specs=[pl.BlockSpec((tm, tn), lambda i, j: (i, j)),
                   pl.BlockSpec((tm, ns), lambda i, j: (i, 0))],
        out_shape=[jax.ShapeDtypeStruct((m, n), F32), jax.ShapeDtypeStruct((m, ns), F32)],
        scratch_shapes=[pltpu.VMEM((tm, k), BF16)],
        compiler_params=_params("parallel", "arbitrary"),
        name="norm_mm",
    )(*x_parts, g.reshape(1, k), w, w_side)


def _hgrn_gates(z, loglb, log1m):
    z2 = z * LOG2E
    l1p = jnp.log2(1.0 + jnp.exp2(-jnp.abs(z2)))
    c = log1m + (jnp.minimum(z2, 0.0) - l1p)
    logf = jnp.maximum(loglb, c) + jnp.log2(1.0 + jnp.exp2(-jnp.abs(loglb - c)))
    return logf, c - z2


def _hgrn_chunk(q, zf, v, cmat, lvl, emat, loglb, log1m, st_ref, cb_ref, reverse):
    c = q.shape[0]
    nv = c // HG_ROWS
    logf, logk = _hgrn_gates(zf, loglb, log1m)

    hi = logf.astype(BF16)
    lo = (logf - hi.astype(F32)).astype(BF16)
    yield
    bb = _dot(cmat, jnp.concatenate([hi, lo], axis=1))
    b = bb[:, :LANES] + bb[:, LANES:]

    def rows(x, p):
        i = nv - 1 - p if reverse else p
        return x[i * HG_ROWS:(i + 1) * HG_ROWS, :]

    def assemble(groups):
        return jnp.concatenate(groups[::-1] if reverse else groups, axis=0)

    edge = 0 if reverse else HG_ROWS - 1
    b_p = [rows(b, p) for p in range(nv)]
    q_p = [rows(q, p) for p in range(nv)]
    lk_p = [rows(logk, p) for p in range(nv)]
    bnd = [jnp.broadcast_to(x[edge:edge + 1, :], (HG_ROWS, LANES)) for x in b_p]
    zero = jnp.zeros((HG_ROWS, LANES), F32)

    a = jnp.zeros((c, c), F32)
    level, mv = 1, nv // 2
    while mv >= 1:
        qm = [zero] * nv
        km = [zero] * nv
        for g in range(nv // (2 * mv)):
            ref = bnd[2 * mv * g + mv - 1]
            for p in range(2 * mv * g, 2 * mv * g + mv):
                km[p] = jnp.exp2(lk_p[p] + (ref - b_p[p]))
            for p in range(2 * mv * g + mv, 2 * mv * (g + 1)):
                qm[p] = q_p[p] * jnp.exp2(b_p[p] - ref)
        yield
        al = _dot_nt(assemble(qm).astype(BF16), assemble(km).astype(BF16))
        a = jnp.where(lvl == float(level), al, a)
        level += 1
        mv //= 2

    cb_ref[...] = b - logk
    slabs = []
    for s in range(HG_ROWS):
        col = []
        for p in range(nv):
            r = (nv - 1 - p if reverse else p) * HG_ROWS + s
            cs = jnp.broadcast_to(cb_ref[r:r + 1, :], (HG_ROWS, LANES))
            col.append(q_p[p] * jnp.exp2(jnp.minimum(b_p[p] - cs, 0.0)))
        slabs.append(assemble(col).astype(BF16))
        yield
    adiag = _dot(jnp.concatenate(slabs, axis=1), emat)
    a = jnp.where(lvl == float(level), adiag, a)

    tot = bnd[nv - 1]
    qbar = assemble([q_p[p] * jnp.exp2(b_p[p]) for p in range(nv)]).astype(BF16)
    kbar = assemble([jnp.exp2(lk_p[p] + (tot - b_p[p])) for p in range(nv)]).astype(BF16)
    vb = v.astype(BF16)
    yield
    st = st_ref[...]
    o = _dot(a.astype(BF16), vb) + _dot_nt(qbar, st.astype(BF16))
    st_ref[...] = jnp.exp2(tot[0:1, :]) * st + _dot_tn(vb, kbar)
    return o


def _lockstep(gens):
    results = [None] * len(gens)
    live = list(range(len(gens)))
    while live:
        for idx in list(live):
            try:
                next(gens[idx])
            except StopIteration as done:
                results[idx] = done.value
                live.remove(idx)
    return results


def _hgrn_kernel(qf_ref, zf_ref, vf_ref, gf_ref, qb_ref, zb_ref, vb_ref, gb_ref,
                 crow_ref, cmat_ref, lvl_ref, emat_ref, o_ref,
                 part_ref, stf_ref, stb_ref, cb_ref, *, step, chunk, nsteps):
    c = pl.program_id(2)

    @pl.when(c == 0)
    def _():
        stf_ref[...] = jnp.zeros_like(stf_ref)
        stb_ref[...] = jnp.zeros_like(stb_ref)
        part_ref[...] = jnp.zeros_like(part_ref)

    crow = crow_ref[0]
    gain = crow[4:5, :]
    emat = emat_ref[...]
    nch = step // chunk

    def emit(o, row, g):
        t = o + part_ref[pl.ds(row, chunk), :]
        part_ref[pl.ds(row, chunk), :] = t
        o_ref[pl.ds(row, chunk), :] = (_rms(t, gain) * (g * _sigmoid(g))).astype(BF16)

    def fwd(j, u):
        rf = pl.multiple_of(j * chunk, chunk)
        sl = pl.ds(rf, chunk)
        gen = _hgrn_chunk(qf_ref[sl, :], zf_ref[sl, :], vf_ref[sl, :], cmat_ref[0], lvl_ref[0], emat,
                          crow[0:1, :], crow[1:2, :], stf_ref, cb_ref.at[0, u], False)
        return gen, pl.multiple_of(c * step + rf, chunk), gf_ref, sl

    def bwd(j, u):
        rb = pl.multiple_of((nch - 1 - j) * chunk, chunk)
        sl = pl.ds(rb, chunk)
        gen = _hgrn_chunk(qb_ref[sl, :], zb_ref[sl, :], vb_ref[sl, :], cmat_ref[1], lvl_ref[1], emat,
                          crow[2:3, :], crow[3:4, :], stb_ref, cb_ref.at[1, u], True)
        return gen, pl.multiple_of((nsteps - 1 - c) * step + rb, chunk), gb_ref, sl

    unroll = math.gcd(HG_UNROLL, nch)

    def body(i, carry):
        chains = [f(i * unroll + u, u) for u in range(unroll) for f in (fwd, bwd)]
        outs = _lockstep([ch[0] for ch in chains])
        for o, (_, row, g_ref, sl) in zip(outs, chains):
            emit(o, row, g_ref[sl, :])
        return carry

    lax.fori_loop(0, nch // unroll, body, 0)


def _hgrn_consts(chunk):
    t = np.arange(chunk)[:, None]
    s = np.arange(chunk)[None, :]
    cm = np.stack([(s <= t), (s >= t)]).astype(np.float32)
    nlev = int(round(math.log2(chunk // HG_ROWS)))
    lv = np.zeros((chunk, chunk), np.float32)
    for level in range(1, nlev + 1):
        m = chunk >> level
        hit = (t // (2 * m) == s // (2 * m)) & (t // m != s // m) & (s < t)
        lv[hit] = level
    lv[((t // HG_ROWS) == (s // HG_ROWS)) & (s <= t)] = nlev + 1
    lvl = np.stack([lv, lv.T]).astype(np.float32)
    em = (np.arange(HG_ROWS * LANES)[:, None] // LANES == (np.arange(chunk)[None, :] % HG_ROWS))
    return jnp.asarray(cm, BF16), jnp.asarray(lvl, F32), jnp.asarray(em.astype(np.float32), BF16)


def _hgrn(z, crow, nb, seq):
    m = z.shape[0]
    step = _pick(seq // 2, HG_STEP, 256, 128, 64)
    chunk = HG_CHUNK
    n = seq // step
    half = n // 2
    cmat, lvl, emat = _hgrn_consts(chunk)

    def zspec(col, fwd, gate=False):
        if fwd:
            blk = (lambda c: jnp.maximum(c, half)) if gate else (lambda c: c)
        else:
            blk = (lambda c: jnp.minimum(n - 1 - c, half - 1)) if gate else (lambda c: n - 1 - c)
        return pl.BlockSpec((step, LANES), lambda b, h, c: (b * n + blk(c), col // LANES + h))

    const3 = lambda shape: pl.BlockSpec(shape, lambda b, h, c: (0, 0, 0))
    return pl.pallas_call(
        functools.partial(_hgrn_kernel, step=step, chunk=chunk, nsteps=n),
        grid=(nb, HEADS, n),
        in_specs=[
            zspec(OFF_Q, True), zspec(OFF_FF, True), zspec(OFF_I, True), zspec(OFF_G, True, True),
            zspec(OFF_Q, False), zspec(OFF_FB, False), zspec(OFF_I, False), zspec(OFF_G, False, True),
            pl.BlockSpec((1, SUBLANES, LANES), lambda b, h, c: (h, 0, 0)),
            const3(cmat.shape), const3(lvl.shape),
            pl.BlockSpec(emat.shape, lambda b, h, c: (0, 0)),
        ],
        out_specs=pl.BlockSpec((seq, LANES), lambda b, h, c: (b, h)),
        out_shape=jax.ShapeDtypeStruct((m, HG_WIDTH), BF16),
        scratch_shapes=[
            pltpu.VMEM((seq, LANES), F32),
            pltpu.VMEM((HEAD_DIM, HEAD_DIM), F32),
            pltpu.VMEM((HEAD_DIM, HEAD_DIM), F32),
            pltpu.VMEM((2, HG_UNROLL, chunk, LANES), F32),
        ],
        compiler_params=_params("parallel", "parallel", "arbitrary"),
        name="hgrn2",
    )(z, z, z, z, z, z, z, z, crow, cmat, lvl, emat)


def _qproj_kernel(z_ref, g_ref, w_ref, cos_ref, sin_ref, o_ref):
    xn = _rms(z_ref[...], g_ref[...]).astype(BF16)
    cz = cos_ref[...]
    sz = sin_ref[...]
    for h in range(HEADS):
        acc = _dot(xn, w_ref[:, h * 3 * LANES:(h + 1) * 3 * LANES])
        o_ref[:, h * Q_PAD:h * Q_PAD + LANES] = (acc[:, :LANES] * QK_SCALE).astype(BF16)
        rope = acc[:, LANES:2 * LANES] * cz + acc[:, 2 * LANES:] * sz
        o_ref[:, h * Q_PAD + LANES:(h + 1) * Q_PAD] = (rope * QK_SCALE).astype(BF16)


def _qproj(z, g, w, cosz, sinz, seq):
    m = z.shape[0]
    tm = _pick(seq, 512, 256, 128)
    ns = seq // tm
    return pl.pallas_call(
        _qproj_kernel,
        grid=(m // tm,),
        in_specs=[
            pl.BlockSpec((tm, LORA), lambda i: (i, OFF_QA // LORA)),
            pl.BlockSpec((1, LORA), lambda i: (0, 0)),
            pl.BlockSpec(w.shape, lambda i: (0, 0)),
            pl.BlockSpec((tm, LANES), lambda i: (i % ns, 0)),
            pl.BlockSpec((tm, LANES), lambda i: (i % ns, 0)),
        ],
        out_specs=pl.BlockSpec((tm, HEADS * Q_PAD), lambda i: (i, 0)),
        out_shape=jax.ShapeDtypeStruct((m, HEADS * Q_PAD), BF16),
        compiler_params=_params("parallel"),
        name="mla_q_proj",
    )(z, g.reshape(1, LORA), w, cosz, sinz)


def _kvproj_kernel(z_ref, kr_ref, g_ref, wk_ref, wvt_ref, cos_ref, sin_ref, k_ref, vt_ref):
    xn = _rms(z_ref[...], g_ref[...]).astype(BF16)
    kr = kr_ref[...]
    krope = (kr[:, :LANES] * cos_ref[...] + kr[:, LANES:] * sin_ref[...]).astype(BF16)
    kn = _dot(xn, wk_ref[...])
    for h in range(HEADS):
        k_ref[:, h * Q_PAD:h * Q_PAD + LANES] = kn[:, h * LANES:(h + 1) * LANES].astype(BF16)
        k_ref[:, h * Q_PAD + LANES:(h + 1) * Q_PAD] = krope
    vt = _dot_nt(wvt_ref[...], xn).astype(BF16)
    ones = jnp.ones((VT_ROWS - HEAD_DIM, vt.shape[1]), BF16)
    for h in range(HEADS):
        vt_ref[0, 0, h, :HEAD_DIM, :] = vt[h * HEAD_DIM:(h + 1) * HEAD_DIM, :]
        vt_ref[0, 0, h, HEAD_DIM:, :] = ones


def _kvproj(z, zkr, g, wk, wvt, cosz, sinz, nb, seq, tk):
    m = z.shape[0]
    ns = seq // tk
    return pl.pallas_call(
        _kvproj_kernel,
        grid=(m // tk,),
        in_specs=[
            pl.BlockSpec((tk, LORA), lambda i: (i, OFF_KVA // LORA)),
            pl.BlockSpec((tk, 2 * LANES), lambda i: (i, 0)),
            pl.BlockSpec((1, LORA), lambda i: (0, 0)),
            pl.BlockSpec(wk.shape, lambda i: (0, 0)),
            pl.BlockSpec(wvt.shape, lambda i: (0, 0)),
            pl.BlockSpec((tk, LANES), lambda i: (i % ns, 0)),
            pl.BlockSpec((tk, LANES), lambda i: (i % ns, 0)),
        ],
        out_specs=[
            pl.BlockSpec((tk, HEADS * Q_PAD), lambda i: (i, 0)),
            pl.BlockSpec((1, 1, HEADS, VT_ROWS, tk), lambda i: (i // ns, i % ns, 0, 0, 0)),
        ],
        out_shape=[
            jax.ShapeDtypeStruct((m, HEADS * Q_PAD), BF16),
            jax.ShapeDtypeStruct((nb, ns, HEADS, VT_ROWS, tk), BF16),
        ],
        compiler_params=_params("parallel"),
        name="mla_kv_proj",
    )(z, zkr, g.reshape(1, LORA), wk, wvt, cosz, sinz)


def _col_reduce(x, op, slab=64):
    r, c = x.shape
    if r > slab and r % slab == 0:
        x = op(x.reshape(r // slab, slab, c), axis=0)
    return op(x, axis=0, keepdims=True)


def _attn_kernel(q_ref, k_ref, vt_ref, o_ref, acc_ref, st_ref, mx_ref, *, tk, nk, nblk):
    tq = q_ref.shape[0]
    nslot, ng, _, gw = st_ref.shape
    assert nblk % nslot == 0 and nk % nblk == 0
    acc_ref[...] = jnp.zeros_like(acc_ref)

    def scores(j, slot, g):
        kb = k_ref[pl.ds(pl.multiple_of(j * tk, tk), tk), :]
        st = _dot_nt(kb, q_ref[g * gw:(g + 1) * gw, :])
        st_ref[slot, g] = st
        mx_ref[slot, g] = _col_reduce(st, jnp.max)

    def consume(j, slot, g, m):
        cols = slice(g * gw, (g + 1) * gw)
        m_new = jnp.maximum(m, mx_ref[slot, g])
        alpha = jnp.exp2(m - m_new)
        p = jnp.exp2((st_ref[slot, g] - m_new).astype(BF16))
        acc_ref[:, cols] = alpha * acc_ref[:, cols] + _dot(vt_ref[0, j, 0], p)
        return m_new

    def body(i, ms):
        j = nblk * i
        ms = list(ms)
        for u in range(nblk):
            nxt = j + u + 1
            if u == nblk - 1:
                nxt = jnp.minimum(nxt, nk - 1)
            for g in range(ng):
                scores(nxt, (u + 1) % nslot, g)
                ms[g] = consume(j + u, u % nslot, g, ms[g])
        return tuple(ms)

    for g in range(ng):
        scores(0, 0, g)
    init = tuple(jnp.full((1, gw), NEG, F32) for _ in range(ng))
    lax.fori_loop(0, nk // nblk, body, init)
    for g in range(ng):
        cols = slice(g * gw, (g + 1) * gw)
        o_ref[cols, :] = (acc_ref[:HEAD_DIM, cols] / acc_ref[HEAD_DIM:HEAD_DIM + 1, cols]).T.astype(BF16)


def _attention(qp, kp, vt, nb, seq, tk):
    m = qp.shape[0]
    tq = _pick(seq, 1024, 512, 256, 128)
    nq = seq // tq
    nk = seq // tk
    nblk = ATT_BLOCKS if nk % ATT_BLOCKS == 0 else 2
    assert nk % nblk == 0, "the key loop handles an even number of blocks per trip"
    gw = min(ATT_GROUP, tq)
    return pl.pallas_call(
        functools.partial(_attn_kernel, tk=tk, nk=nk, nblk=nblk),
        grid=(nb, HEADS, nq),
        in_specs=[
            pl.BlockSpec((tq, Q_PAD), lambda b, h, i: (b * nq + i, h)),
            pl.BlockSpec((seq, Q_PAD), lambda b, h, i: (b, h)),
            pl.BlockSpec((1, nk, 1, VT_ROWS, tk), lambda b, h, i: (b, 0, h, 0, 0)),
        ],
        out_specs=pl.BlockSpec((tq, HEAD_DIM), lambda b, h, i: (b * nq + i, h)),
        out_shape=jax.ShapeDtypeStruct((m, HG_WIDTH), BF16),
        scratch_shapes=[pltpu.VMEM((VT_ROWS, tq), F32), pltpu.VMEM((2, tq // gw, tk, gw), F32),
                        pltpu.VMEM((2, tq // gw, 1, gw), F32)],
        compiler_params=_params("parallel", "parallel", "arbitrary"),
        name="mla_attention",
    )(qp, kp, vt)


def _mm_res_kernel(*refs, n_in, tiles):
    res_refs, o_ref = refs[2 * n_in:-1], refs[-1]
    acc = _part_read(res_refs, tiles, pl.program_id(0))
    for a_ref, w_ref in zip(refs[:n_in], refs[n_in:2 * n_in]):
        acc = acc + _dot(a_ref[...], w_ref[...])
    o_ref[...] = acc


def _mm_res(a_list, w, res_parts, tm, tn):
    n = w.shape[1]
    tiles = _part_tiles(res_parts, tm)
    m = sum(tiles) * tm
    n_in = len(a_list)
    kb = a_list[0].shape[1]
    assert all(a.shape == (m, kb) for a in a_list) and w.shape[0] == n_in * kb
    w_list = [w] * n_in
    in_specs = [pl.BlockSpec((tm, kb), lambda i, j: (i, 0)) for _ in a_list]
    in_specs += [pl.BlockSpec((kb, tn), functools.partial(lambda i, j, r: (r, j), r=r))
                 for r in range(n_in)]
    in_specs += _part_specs(res_parts, tm, tn, lambda i, j: j)
    return pl.pallas_call(
        functools.partial(_mm_res_kernel, n_in=n_in, tiles=tiles),
        grid=(m // tm, n // tn),
        in_specs=in_specs,
        out_specs=pl.BlockSpec((tm, tn), lambda i, j: (i, j)),
        out_shape=jax.ShapeDtypeStruct((m, n), F32),
        compiler_params=_params("parallel", "arbitrary"),
        name="mm_residual",
    )(*a_list, *w_list, *res_parts)


HALO = 16


def _ffn_up_kernel(x_ref, xp_ref, xn_ref, g_ref, wg_ref, wu_ref, cw_ref, cb_ref, o_ref, n_ref,
                   *, tm, tiles_per_seq):
    i = pl.program_id(0)

    @pl.when(pl.program_id(1) == 0)
    def _():
        g = g_ref[...]
        first = (i % tiles_per_seq) == 0
        last = (i % tiles_per_seq) == tiles_per_seq - 1
        n_ref[0:HALO, :] = jnp.where(first, 0.0, _rms(xp_ref[...], g)).astype(BF16)
        n_ref[HALO:HALO + tm, :] = _rms(x_ref[...], g).astype(BF16)
        n_ref[HALO + tm:, :] = jnp.where(last, 0.0, _rms(xn_ref[...], g)).astype(BF16)

    ge = _dot(n_ref[...], wg_ref[...])
    rows = tm + 2 * HALO
    prev = pltpu.roll(ge, 1, 0)[HALO:HALO + tm, :]
    nxt = pltpu.roll(ge, rows - 1, 0)[HALO:HALO + tm, :]
    cw = cw_ref[...]
    gate = prev * cw[0:1, :] + ge[HALO:HALO + tm, :] * cw[1:2, :] + nxt * cw[2:3, :] + cb_ref[...]
    up = _dot(n_ref[HALO:HALO + tm, :], wu_ref[...])
    o_ref[...] = (gate * _sigmoid(gate) * up).astype(BF16)


def _ffn_up(h, g, w_up, conv_w, conv_b, seq):
    m, k = h.shape
    tm = _pick(seq, 1024, 512, 256, 128)
    tf = 512
    nf = D_FF // tf
    tps = seq // tm
    r = tm // HALO
    nblk = m // HALO
    return pl.pallas_call(
        functools.partial(_ffn_up_kernel, tm=tm, tiles_per_seq=tps),
        grid=(m // tm, nf),
        in_specs=[
            pl.BlockSpec((tm, k), lambda i, j: (i, 0)),
            pl.BlockSpec((HALO, k), lambda i, j: (jnp.maximum(i * r - 1, 0), 0)),
            pl.BlockSpec((HALO, k), lambda i, j: (jnp.minimum((i + 1) * r, nblk - 1), 0)),
            pl.BlockSpec((1, k), lambda i, j: (0, 0)),
            pl.BlockSpec((k, tf), lambda i, j: (0, j)),
            pl.BlockSpec((k, tf), lambda i, j: (0, j + nf)),
            pl.BlockSpec((3, tf), lambda i, j: (0, j)),
            pl.BlockSpec((1, tf), lambda i, j: (0, j)),
        ],
        out_specs=pl.BlockSpec((tm, tf), lambda i, j: (i, j)),
        out_shape=jax.ShapeDtypeStruct((m, D_FF), BF16),
        scratch_shapes=[pltpu.VMEM((tm + 2 * HALO, k), BF16)],
        compiler_params=_params("parallel", "arbitrary"),
        name="ffn_up",
    )(h, h, h, g.reshape(1, k), w_up, w_up, conv_w, conv_b.reshape(1, D_FF))


def _ple_kernel(h_ref, *refs, split, tn, p_tiles):
    p_refs = refs[:len(p_tiles)]
    g_ref, wg_ref, wp_ref, fn_ref = refs[len(p_tiles):len(p_tiles) + 4]
    out_refs = refs[len(p_tiles) + 4:]
    x = h_ref[...]
    xn = _rms(x, g_ref[...]).astype(BF16)
    pb = _part_read(p_refs, p_tiles, pl.program_id(0)).astype(BF16)
    y_ref = out_refs[-1]
    for c in range(D_MODEL // tn):
        sl = slice(c * tn, (c + 1) * tn)
        gate = _dot(xn, wg_ref[:, sl])
        y_ref[:, sl] = x[:, sl] + _dot(pb, wp_ref[:, sl]) * _sigmoid(gate)
    if split is not None:
        first_ref, second_ref, _ = out_refs
        y = _rms(y_ref[...], fn_ref[...])
        i = pl.program_id(0)

        @pl.when(i < split)
        def _():
            first_ref[...] = y

        @pl.when(i >= split)
        def _():
            second_ref[...] = y


def _ple(h, p_parts, layer, g, wg, wp, fn, m_first=None):
    m, k = h.shape
    tm = _pick(math.gcd(*[p.shape[0] // DEPTH for p in p_parts]), 512, 256, 128)
    p_tiles = tuple(p.shape[0] // DEPTH // tm for p in p_parts)
    assert sum(p_tiles) * tm == m and (m_first is None or m_first % tm == 0)
    p_specs, start = [], 0
    for nt in p_tiles:
        p_specs.append(pl.BlockSpec((tm, PLE_DIM), functools.partial(
            lambda i, s, nt: (layer * nt + jnp.clip(i - s, 0, nt - 1), 0), s=start, nt=nt)))
        start += nt
    tile = pl.BlockSpec((tm, k), lambda i: (i, 0))
    if m_first is None:
        split, scratch = None, []
        out_specs, out_shape = tile, jax.ShapeDtypeStruct((m, k), F32)
    else:
        split, scratch = m_first // tm, [pltpu.VMEM((tm, k), F32)]
        out_specs = [pl.BlockSpec((tm, k), lambda i: (jnp.minimum(i, split - 1), 0)),
                     pl.BlockSpec((tm, k), lambda i: (jnp.maximum(i - split, 0), 0))]
        out_shape = [jax.ShapeDtypeStruct((m_first, k), F32), jax.ShapeDtypeStruct((m - m_first, k), F32)]
    return pl.pallas_call(
        functools.partial(_ple_kernel, split=split, tn=512, p_tiles=p_tiles),
        grid=(m // tm,),
        in_specs=[tile] + p_specs + [
            pl.BlockSpec((1, k), lambda i: (0, 0)),
            pl.BlockSpec(wg.shape, lambda i: (0, 0)),
            pl.BlockSpec(wp.shape, lambda i: (0, 0)),
            pl.BlockSpec((1, k), lambda i: (0, 0)),
        ],
        out_specs=out_specs,
        out_shape=out_shape,
        scratch_shapes=scratch,
        compiler_params=_params("arbitrary"),
        name="ple_gate",
    )(h, *p_parts, g.reshape(1, k), wg, wp, fn.reshape(1, k))


def _rot_cols(w):
    half = ROPE_DIM // 2
    return jnp.concatenate([-w[..., half:], w[..., :half]], axis=-1)


def _pad_rope(w):
    return jnp.concatenate([w, jnp.zeros_like(w)], axis=-1)


def _prepare(attn_norm, w_in, hg_lower, hg_norm, w_q_b, w_kv_b, w_out, w_up, w_down,
             w_ple_gate, w_ple_proj):
    w_kr = w_in[:, :, OFF_KR:]
    wq = w_q_b.reshape(DEPTH, LORA, HEADS, HEAD_DIM + ROPE_DIM)
    wq_r = wq[..., HEAD_DIM:]
    wq = jnp.concatenate([wq[..., :HEAD_DIM], _pad_rope(wq_r), _pad_rope(_rot_cols(wq_r))], axis=-1)
    wkv = w_kv_b.reshape(DEPTH, LORA, HEADS, 2 * HEAD_DIM)

    lb = jnp.cumsum(jax.nn.softmax(hg_lower.astype(F32), axis=1), axis=1)
    lb = jnp.maximum(lb - lb[:, :1], 0.0)
    loglb = jnp.maximum(jnp.log(lb) * LOG2E, NEG).reshape(2, DEPTH, HEADS, HEAD_DIM)
    log1m = (jnp.log1p(-lb) * LOG2E).reshape(2, DEPTH, HEADS, HEAD_DIM)
    gain = hg_norm.astype(F32).reshape(DEPTH, HEADS, HEAD_DIM)
    zeros = jnp.zeros_like(gain)
    crow = jnp.stack([loglb[0], log1m[0], loglb[1], log1m[1], gain, zeros, zeros, zeros], axis=2)

    return dict(
        w_in=w_in[:, :, :OFF_KR].astype(BF16),
        w_kr=jnp.concatenate([_pad_rope(w_kr), _pad_rope(_rot_cols(w_kr))], axis=-1).astype(BF16),
        w_q=wq.reshape(DEPTH, LORA, HEADS * 3 * LANES).astype(BF16),
        w_k=wkv[..., :HEAD_DIM].reshape(DEPTH, LORA, HG_WIDTH).astype(BF16),
        w_vt=jnp.swapaxes(wkv[..., HEAD_DIM:].reshape(DEPTH, LORA, HG_WIDTH), 1, 2).astype(BF16),
        w_out=w_out.astype(BF16),
        w_up=w_up.astype(BF16),
        w_down=w_down.astype(BF16),
        w_ple_gate=w_ple_gate.astype(BF16),
        w_ple_proj=w_ple_proj.astype(BF16),
        crow=crow,
    )


def _rope_tables(seq):
    inv = ROPE_THETA ** (-jnp.arange(0, ROPE_DIM, 2, dtype=F32) / ROPE_DIM)
    ang = jnp.arange(seq, dtype=F32)[:, None] * inv[None, :]
    pad = jnp.zeros((seq, LANES - ROPE_DIM), F32)
    cosz = jnp.concatenate([jnp.cos(ang), jnp.cos(ang), pad], axis=-1)
    sinz = jnp.concatenate([jnp.sin(ang), jnp.sin(ang), pad], axis=-1)
    return cosz, sinz


def _trunk(xs, ps, attn_norm, q_a_norm, kv_a_norm, ffn_norm, conv_w, conv_b, ple_norm, final_norm, prm):
    seq, d = xs[0].shape[1:]
    nbs = [x.shape[0] for x in xs]
    nb = sum(nbs)
    h_parts = [x.reshape(n * seq, d) for x, n in zip(xs, nbs)]
    p_parts = [p.reshape(DEPTH * n * seq, PLE_DIM) for p, n in zip(ps, nbs)]
    cosz, sinz = _rope_tables(seq)
    tm = _pick(seq, 1024, 512, 256, 128)
    tk = _pick(seq, 1024, 512, 256, 128)
    for l in range(DEPTH):
        z, zkr = _norm_mm(h_parts, attn_norm[l], prm["w_in"][l], prm["w_kr"][l], tm,
                          1024 // len(h_parts))
        o_hg = _hgrn(z, prm["crow"][l], nb, seq)
        qp = _qproj(z, q_a_norm[l], prm["w_q"][l], cosz, sinz, seq)
        kp, vt = _kvproj(z, zkr, kv_a_norm[l], prm["w_k"][l], prm["w_vt"][l], cosz, sinz, nb, seq, tk)
        o_mla = _attention(qp, kp, vt, nb, seq, tk)
        h = _mm_res([o_hg, o_mla], prm["w_out"][l], h_parts, tm, 512)
        act = _ffn_up(h, ffn_norm[l], prm["w_up"][l], conv_w[l], conv_b[l], seq)
        h = _mm_res([act], prm["w_down"][l], [h], tm, 512)
        h = _ple(h, p_parts, l, ple_norm[l], prm["w_ple_gate"][l], prm["w_ple_proj"][l], final_norm,
                 m_first=nbs[0] * seq if l == DEPTH - 1 else None)
        h_parts = [h]
    return h[0].reshape(nbs[0], seq, d), h[1].reshape(nbs[1], seq, d)


def kernel(x_prompt, x_sample, p_prompt, p_sample, attn_norm, w_in, hg_lower, hg_norm, q_a_norm,
           w_q_b, kv_a_norm, w_kv_b, w_out, ffn_norm, w_up, conv_w, conv_b, w_down, ple_norm,
           w_ple_gate, w_ple_proj, final_norm):
    assert x_prompt.shape[1:] == x_sample.shape[1:], "both groups must share the sequence length"
    prm = _prepare(attn_norm, w_in, hg_lower, hg_norm, w_q_b, w_kv_b, w_out, w_up, w_down,
                   w_ple_gate, w_ple_proj)
    return _trunk([x_prompt, x_sample], [p_prompt, p_sample], attn_norm, q_a_norm, kv_a_norm,
                  ffn_norm, conv_w, conv_b, ple_norm, final_norm, prm)
```

```python
import functools
import math

import jax
import jax.numpy as jnp
import numpy as np
from jax import lax
from jax.experimental import pallas as pl
from jax.experimental.pallas import tpu as pltpu

F32 = jnp.float32
BF16 = jnp.bfloat16

D_MODEL = 2048
DEPTH = 4
HEADS = 8
HEAD_DIM = 128
ROPE_DIM = 64
LORA = 512
HG_WIDTH = HEADS * HEAD_DIM
OFF_Q, OFF_FF, OFF_FB, OFF_I, OFF_G, OFF_QA, OFF_KVA, OFF_KR = (
    0, 1024, 2048, 3072, 4096, 5120, 5632, 6144)
D_FF = 5632
PLE_DIM = 256
EPS = 1e-6
ROPE_THETA = 10000.0
NEG = -1e30

LANES = 128
SUBLANES = 8
VMEM_LIMIT = 56 * 2**20

HG_CHUNK = 64
HG_ROWS = SUBLANES
HG_STEP = 512
HG_UNROLL = 8
Q_PAD = 2 * LANES
LOG2E = math.log2(math.e)
QK_SCALE = (HEAD_DIM + ROPE_DIM) ** -0.5 * LOG2E
ATT_GROUP = 2 * LANES
ATT_BLOCKS = 2
VT_ROWS = HEAD_DIM + 16


def _params(*sem):
    return pltpu.CompilerParams(dimension_semantics=sem, vmem_limit_bytes=VMEM_LIMIT)


def _pick(n, *cands):
    for c in cands:
        if n % c == 0:
            return c
    raise ValueError(f"no tile for {n} in {cands}")


def _rms(x, g):
    ms = jnp.mean(x * x, axis=-1, keepdims=True)
    return x * lax.rsqrt(ms + EPS) * g


def _sigmoid(x):
    return 1.0 / (1.0 + jnp.exp(-x))


def _dot(a, b):
    return jnp.dot(a, b, preferred_element_type=F32)


def _dot_nt(a, b):
    return lax.dot_general(a, b, (((1,), (1,)), ((), ())), preferred_element_type=F32)


def _dot_tn(a, b):
    return lax.dot_general(a, b, (((0,), (0,)), ((), ())), preferred_element_type=F32)


def _part_tiles(parts, tm):
    assert all(p.shape[0] % tm == 0 for p in parts)
    return tuple(p.shape[0] // tm for p in parts)


def _part_specs(parts, tm, width, col):
    specs, start = [], 0
    for nt in _part_tiles(parts, tm):
        def imap(i, *rest, start=start, nt=nt):
            inside = (i >= start) & (i < start + nt)
            return jnp.clip(i - start, 0, nt - 1), jnp.where(inside, col(i, *rest), 0)
        specs.append(pl.BlockSpec((tm, width), imap))
        start += nt
    return specs


def _part_read(refs, tiles, i):
    x, start = refs[0][...], tiles[0]
    for ref, nt in zip(refs[1:], tiles[1:]):
        x = jnp.where(i >= start, ref[...], x)
        start += nt
    return x


def _norm_mm_kernel(*refs, tiles):
    x_refs = refs[:len(tiles)]
    g_ref, w_ref, ws_ref, o_ref, os_ref, xn_ref = refs[len(tiles):]

    @pl.when(pl.program_id(1) == 0)
    def _():
        x = _part_read(x_refs, tiles, pl.program_id(0))
        xn_ref[...] = _rms(x, g_ref[...]).astype(BF16)
        os_ref[...] = _dot(xn_ref[...], ws_ref[...])

    o_ref[...] = _dot(xn_ref[...], w_ref[...])


def _norm_mm(x_parts, g, w, w_side, tm, tn):
    k, n = w.shape
    ns = w_side.shape[1]
    tiles = _part_tiles(x_parts, tm)
    m = sum(tiles) * tm
    return pl.pallas_call(
        functools.partial(_norm_mm_kernel, tiles=tiles),
        grid=(m // tm, n // tn),
        in_specs=_part_specs(x_parts, tm, k, lambda i, j: 0) + [
            pl.BlockSpec((1, k), lambda i, j: (0, 0)),
            pl.BlockSpec((k, tn), lambda i, j: (0, j)),
            pl.BlockSpec((k, ns), lambda i, j: (0, 0)),
        ],
        out_specs=[pl.BlockSpec((tm, tn), lambda i, j: (i, j)),
                   pl.BlockSpec((tm, ns), lambda i, j: (i, 0))],
        out_shape=[jax.ShapeDtypeStruct((m, n), F32), jax.ShapeDtypeStruct((m, ns), F32)],
        scratch_shapes=[pltpu.VMEM((tm, k), BF16)],
        compiler_params=_params("parallel", "arbitrary"),
        name="norm_mm",
    )(*x_parts, g.reshape(1, k), w, w_side)


def _hgrn_gates(z, loglb, log1m):
    z2 = z * LOG2E
    l1p = jnp.log2(1.0 + jnp.exp2(-jnp.abs(z2)))
    c = log1m + (jnp.minimum(z2, 0.0) - l1p)
    logf = jnp.maximum(loglb, c) + jnp.log2(1.0 + jnp.exp2(-jnp.abs(loglb - c)))
    return logf, c - z2


def _hgrn_chunk(q, zf, v, cmat, lvl, emat, loglb, log1m, st_ref, cb_ref, reverse):
    c = q.shape[0]
    nv = c // HG_ROWS
    logf, logk = _hgrn_gates(zf, loglb, log1m)

    hi = logf.astype(BF16)
    lo = (logf - hi.astype(F32)).astype(BF16)
    yield
    bb = _dot(cmat, jnp.concatenate([hi, lo], axis=1))
    b = bb[:, :LANES] + bb[:, LANES:]

    def rows(x, p):
        i = nv - 1 - p if reverse else p
        return x[i * HG_ROWS:(i + 1) * HG_ROWS, :]

    def assemble(groups):
        return jnp.concatenate(groups[::-1] if reverse else groups, axis=0)

    edge = 0 if reverse else HG_ROWS - 1
    b_p = [rows(b, p) for p in range(nv)]
    q_p = [rows(q, p) for p in range(nv)]
    lk_p = [rows(logk, p) for p in range(nv)]
    bnd = [jnp.broadcast_to(x[edge:edge + 1, :], (HG_ROWS, LANES)) for x in b_p]
    zero = jnp.zeros((HG_ROWS, LANES), F32)

    a = jnp.zeros((c, c), F32)
    level, mv = 1, nv // 2
    while mv >= 1:
        qm = [zero] * nv
        km = [zero] * nv
        for g in range(nv // (2 * mv)):
            ref = bnd[2 * mv * g + mv - 1]
            for p in range(2 * mv * g, 2 * mv * g + mv):
                km[p] = jnp.exp2(lk_p[p] + (ref - b_p[p]))
            for p in range(2 * mv * g + mv, 2 * mv * (g + 1)):
                qm[p] = q_p[p] * jnp.exp2(b_p[p] - ref)
        yield
        al = _dot_nt(assemble(qm).astype(BF16), assemble(km).astype(BF16))
        a = jnp.where(lvl == float(level), al, a)
        level += 1
        mv //= 2

    cb_ref[...] = b - logk
    slabs = []
    for s in range(HG_ROWS):
        col = []
        for p in range(nv):
            r = (nv - 1 - p if reverse else p) * HG_ROWS + s
            cs = jnp.broadcast_to(cb_ref[r:r + 1, :], (HG_ROWS, LANES))
            col.append(q_p[p] * jnp.exp2(jnp.minimum(b_p[p] - cs, 0.0)))
        slabs.append(assemble(col).astype(BF16))
        yield
    adiag = _dot(jnp.concatenate(slabs, axis=1), emat)
    a = jnp.where(lvl == float(level), adiag, a)

    tot = bnd[nv - 1]
    qbar = assemble([q_p[p] * jnp.exp2(b_p[p]) for p in range(nv)]).astype(BF16)
    kbar = assemble([jnp.exp2(lk_p[p] + (tot - b_p[p])) for p in range(nv)]).astype(BF16)
    vb = v.astype(BF16)
    yield
    st = st_ref[...]
    o = _dot(a.astype(BF16), vb) + _dot_nt(qbar, st.astype(BF16))
    st_ref[...] = jnp.exp2(tot[0:1, :]) * st + _dot_tn(vb, kbar)
    return o


def _lockstep(gens):
    results = [None] * len(gens)
    live = list(range(len(gens)))
    while live:
        for idx in list(live):
            try:
                next(gens[idx])
            except StopIteration as done:
                results[idx] = done.value
                live.remove(idx)
    return results


def _hgrn_kernel(qf_ref, zf_ref, vf_ref, gf_ref, qb_ref, zb_ref, vb_ref, gb_ref,
                 crow_ref, cmat_ref, lvl_ref, emat_ref, o_ref,
                 part_ref, stf_ref, stb_ref, cb_ref, *, step, chunk, nsteps):
    c = pl.program_id(2)

    @pl.when(c == 0)
    def _():
        stf_ref[...] = jnp.zeros_like(stf_ref)
        stb_ref[...] = jnp.zeros_like(stb_ref)
        part_ref[...] = jnp.zeros_like(part_ref)

    crow = crow_ref[0]
    gain = crow[4:5, :]
    emat = emat_ref[...]
    nch = step // chunk

    def emit(o, row, g):
        t = o + part_ref[pl.ds(row, chunk), :]
        part_ref[pl.ds(row, chunk), :] = t
        o_ref[pl.ds(row, chunk), :] = (_rms(t, gain) * (g * _sigmoid(g))).astype(BF16)

    def fwd(j, u):
        rf = pl.multiple_of(j * chunk, chunk)
        sl = pl.ds(rf, chunk)
        gen = _hgrn_chunk(qf_ref[sl, :], zf_ref[sl, :], vf_ref[sl, :], cmat_ref[0], lvl_ref[0], emat,
                          crow[0:1, :], crow[1:2, :], stf_ref, cb_ref.at[0, u], False)
        return gen, pl.multiple_of(c * step + rf, chunk), gf_ref, sl

    def bwd(j, u):
        rb = pl.multiple_of((nch - 1 - j) * chunk, chunk)
        sl = pl.ds(rb, chunk)
        gen = _hgrn_chunk(qb_ref[sl, :], zb_ref[sl, :], vb_ref[sl, :], cmat_ref[1], lvl_ref[1], emat,
                          crow[2:3, :], crow[3:4, :], stb_ref, cb_ref.at[1, u], True)
        return gen, pl.multiple_of((nsteps - 1 - c) * step + rb, chunk), gb_ref, sl

    unroll = math.gcd(HG_UNROLL, nch)

    def body(i, carry):
        chains = [f(i * unroll + u, u) for u in range(unroll) for f in (fwd, bwd)]
        outs = _lockstep([ch[0] for ch in chains])
        for o, (_, row, g_ref, sl) in zip(outs, chains):
            emit(o, row, g_ref[sl, :])
        return carry

    lax.fori_loop(0, nch // unroll, body, 0)


def _hgrn_consts(chunk):
    t = np.arange(chunk)[:, None]
    s = np.arange(chunk)[None, :]
    cm = np.stack([(s <= t), (s >= t)]).astype(np.float32)
    nlev = int(round(math.log2(chunk // HG_ROWS)))
    lv = np.zeros((chunk, chunk), np.float32)
    for level in range(1, nlev + 1):
        m = chunk >> level
        hit = (t // (2 * m) == s // (2 * m)) & (t // m != s // m) & (s < t)
        lv[hit] = level
    lv[((t // HG_ROWS) == (s // HG_ROWS)) & (s <= t)] = nlev + 1
    lvl = np.stack([lv, lv.T]).astype(np.float32)
    em = (np.arange(HG_ROWS * LANES)[:, None] // LANES == (np.arange(chunk)[None, :] % HG_ROWS))
    return jnp.asarray(cm, BF16), jnp.asarray(lvl, F32), jnp.asarray(em.astype(np.float32), BF16)


def _hgrn(z, crow, nb, seq):
    m = z.shape[0]
    step = _pick(seq // 2, HG_STEP, 256, 128, 64)
    chunk = HG_CHUNK
    n = seq // step
    half = n // 2
    cmat, lvl, emat = _hgrn_consts(chunk)

    def zspec(col, fwd, gate=False):
        if fwd:
            blk = (lambda c: jnp.maximum(c, half)) if gate else (lambda c: c)
        else:
            blk = (lambda c: jnp.minimum(n - 1 - c, half - 1)) if gate else (lambda c: n - 1 - c)
        return pl.BlockSpec((step, LANES), lambda b, h, c: (b * n + blk(c), col // LANES + h))

    const3 = lambda shape: pl.BlockSpec(shape, lambda b, h, c: (0, 0, 0))
    return pl.pallas_call(
        functools.partial(_hgrn_kernel, step=step, chunk=chunk, nsteps=n),
        grid=(nb, HEADS, n),
        in_specs=[
            zspec(OFF_Q, True), zspec(OFF_FF, True), zspec(OFF_I, True), zspec(OFF_G, True, True),
            zspec(OFF_Q, False), zspec(OFF_FB, False), zspec(OFF_I, False), zspec(OFF_G, False, True),
            pl.BlockSpec((1, SUBLANES, LANES), lambda b, h, c: (h, 0, 0)),
            const3(cmat.shape), const3(lvl.shape),
            pl.BlockSpec(emat.shape, lambda b, h, c: (0, 0)),
        ],
        out_specs=pl.BlockSpec((seq, LANES), lambda b, h, c: (b, h)),
        out_shape=jax.ShapeDtypeStruct((m, HG_WIDTH), BF16),
        scratch_shapes=[
            pltpu.VMEM((seq, LANES), F32),
            pltpu.VMEM((HEAD_DIM, HEAD_DIM), F32),
            pltpu.VMEM((HEAD_DIM, HEAD_DIM), F32),
            pltpu.VMEM((2, HG_UNROLL, chunk, LANES), F32),
        ],
        compiler_params=_params("parallel", "parallel", "arbitrary"),
        name="hgrn2",
    )(z, z, z, z, z, z, z, z, crow, cmat, lvl, emat)


def _qproj_kernel(z_ref, g_ref, w_ref, cos_ref, sin_ref, o_ref):
    xn = _rms(z_ref[...], g_ref[...]).astype(BF16)
    cz = cos_ref[...]
    sz = sin_ref[...]
    for h in range(HEADS):
        acc = _dot(xn, w_ref[:, h * 3 * LANES:(h + 1) * 3 * LANES])
        o_ref[:, h * Q_PAD:h * Q_PAD + LANES] = (acc[:, :LANES] * QK_SCALE).astype(BF16)
        rope = acc[:, LANES:2 * LANES] * cz + acc[:, 2 * LANES:] * sz
        o_ref[:, h * Q_PAD + LANES:(h + 1) * Q_PAD] = (rope * QK_SCALE).astype(BF16)


def _qproj(z, g, w, cosz, sinz, seq):
    m = z.shape[0]
    tm = _pick(seq, 512, 256, 128)
    ns = seq // tm
    return pl.pallas_call(
        _qproj_kernel,
        grid=(m // tm,),
        in_specs=[
            pl.BlockSpec((tm, LORA), lambda i: (i, OFF_QA // LORA)),
            pl.BlockSpec((1, LORA), lambda i: (0, 0)),
            pl.BlockSpec(w.shape, lambda i: (0, 0)),
            pl.BlockSpec((tm, LANES), lambda i: (i % ns, 0)),
            pl.BlockSpec((tm, LANES), lambda i: (i % ns, 0)),
        ],
        out_specs=pl.BlockSpec((tm, HEADS * Q_PAD), lambda i: (i, 0)),
        out_shape=jax.ShapeDtypeStruct((m, HEADS * Q_PAD), BF16),
        compiler_params=_params("parallel"),
        name="mla_q_proj",
    )(z, g.reshape(1, LORA), w, cosz, sinz)


def _kvproj_kernel(z_ref, kr_ref, g_ref, wk_ref, wvt_ref, cos_ref, sin_ref, k_ref, vt_ref):
    xn = _rms(z_ref[...], g_ref[...]).astype(BF16)
    kr = kr_ref[...]
    krope = (kr[:, :LANES] * cos_ref[...] + kr[:, LANES:] * sin_ref[...]).astype(BF16)
    kn = _dot(xn, wk_ref[...])
    for h in range(HEADS):
        k_ref[:, h * Q_PAD:h * Q_PAD + LANES] = kn[:, h * LANES:(h + 1) * LANES].astype(BF16)
        k_ref[:, h * Q_PAD + LANES:(h + 1) * Q_PAD] = krope
    vt = _dot_nt(wvt_ref[...], xn).astype(BF16)
    ones = jnp.ones((VT_ROWS - HEAD_DIM, vt.shape[1]), BF16)
    for h in range(HEADS):
        vt_ref[0, 0, h, :HEAD_DIM, :] = vt[h * HEAD_DIM:(h + 1) * HEAD_DIM, :]
        vt_ref[0, 0, h, HEAD_DIM:, :] = ones


def _kvproj(z, zkr, g, wk, wvt, cosz, sinz, nb, seq, tk):
    m = z.shape[0]
    ns = seq // tk
    return pl.pallas_call(
        _kvproj_kernel,
        grid=(m // tk,),
        in_specs=[
            pl.BlockSpec((tk, LORA), lambda i: (i, OFF_KVA // LORA)),
            pl.BlockSpec((tk, 2 * LANES), lambda i: (i, 0)),
            pl.BlockSpec((1, LORA), lambda i: (0, 0)),
            pl.BlockSpec(wk.shape, lambda i: (0, 0)),
            pl.BlockSpec(wvt.shape, lambda i: (0, 0)),
            pl.BlockSpec((tk, LANES), lambda i: (i % ns, 0)),
            pl.BlockSpec((tk, LANES), lambda i: (i % ns, 0)),
        ],
        out_specs=[
            pl.BlockSpec((tk, HEADS * Q_PAD), lambda i: (i, 0)),
            pl.BlockSpec((1, 1, HEADS, VT_ROWS, tk), lambda i: (i // ns, i % ns, 0, 0, 0)),
        ],
        out_shape=[
            jax.ShapeDtypeStruct((m, HEADS * Q_PAD), BF16),
            jax.ShapeDtypeStruct((nb, ns, HEADS, VT_ROWS, tk), BF16),
        ],
        compiler_params=_params("parallel"),
        name="mla_kv_proj",
    )(z, zkr, g.reshape(1, LORA), wk, wvt, cosz, sinz)


def _col_reduce(x, op, slab=64):
    r, c = x.shape
    if r > slab and r % slab == 0:
        x = op(x.reshape(r // slab, slab, c), axis=0)
    return op(x, axis=0, keepdims=True)


def _attn_kernel(q_ref, k_ref, vt_ref, o_ref, acc_ref, st_ref, mx_ref, *, tk, nk, nblk):
    tq = q_ref.shape[0]
    nslot, ng, _, gw = st_ref.shape
    assert nblk % nslot == 0 and nk % nblk == 0
    acc_ref[...] = jnp.zeros_like(acc_ref)

    def scores(j, slot, g):
        kb = k_ref[pl.ds(pl.multiple_of(j * tk, tk), tk), :]
        st = _dot_nt(kb, q_ref[g * gw:(g + 1) * gw, :])
        st_ref[slot, g] = st
        mx_ref[slot, g] = _col_reduce(st, jnp.max)

    def consume(j, slot, g, m):
        cols = slice(g * gw, (g + 1) * gw)
        m_new = jnp.maximum(m, mx_ref[slot, g])
        alpha = jnp.exp2(m - m_new)
        p = jnp.exp2((st_ref[slot, g] - m_new).astype(BF16))
        acc_ref[:, cols] = alpha * acc_ref[:, cols] + _dot(vt_ref[0, j, 0], p)
        return m_new

    def trip(j, ms, last):
        ms = list(ms)
        for u in range(nblk):
            for g in range(ng):
                if not (last and u == nblk - 1):
                    scores(j + u + 1, (u + 1) % nslot, g)
                ms[g] = consume(j + u, u % nslot, g, ms[g])
        return tuple(ms)

    for g in range(ng):
        scores(0, 0, g)
    init = tuple(jnp.full((1, gw), NEG, F32) for _ in range(ng))
    ms = lax.fori_loop(0, nk // nblk - 1, lambda i, ms: trip(nblk * i, ms, False), init)
    trip(nk - nblk, ms, True)
    for g in range(ng):
        cols = slice(g * gw, (g + 1) * gw)
        o_ref[cols, :] = (acc_ref[:HEAD_DIM, cols] / acc_ref[HEAD_DIM:HEAD_DIM + 1, cols]).T.astype(BF16)


def _attention(qp, kp, vt, nb, seq, tk):
    m = qp.shape[0]
    tq = _pick(seq, 1024, 512, 256, 128)
    nq = seq // tq
    nk = seq // tk
    nblk = ATT_BLOCKS if nk % ATT_BLOCKS == 0 else 2
    assert nk % nblk == 0, "the key loop handles an even number of blocks per trip"
    gw = min(ATT_GROUP, tq)
    return pl.pallas_call(
        functools.partial(_attn_kernel, tk=tk, nk=nk, nblk=nblk),
        grid=(nb, HEADS, nq),
        in_specs=[
            pl.BlockSpec((tq, Q_PAD), lambda b, h, i: (b * nq + i, h)),
            pl.BlockSpec((seq, Q_PAD), lambda b, h, i: (b, h)),
            pl.BlockSpec((1, nk, 1, VT_ROWS, tk), lambda b, h, i: (b, 0, h, 0, 0)),
        ],
        out_specs=pl.BlockSpec((tq, HEAD_DIM), lambda b, h, i: (b * nq + i, h)),
        out_shape=jax.ShapeDtypeStruct((m, HG_WIDTH), BF16),
        scratch_shapes=[pltpu.VMEM((VT_ROWS, tq), F32), pltpu.VMEM((2, tq // gw, tk, gw), F32),
                        pltpu.VMEM((2, tq // gw, 1, gw), F32)],
        compiler_params=_params("parallel", "parallel", "arbitrary"),
        name="mla_attention",
    )(qp, kp, vt)


def _mm_res_kernel(*refs, n_in, tiles):
    res_refs, o_ref = refs[2 * n_in:-1], refs[-1]
    acc = _part_read(res_refs, tiles, pl.program_id(0))
    for a_ref, w_ref in zip(refs[:n_in], refs[n_in:2 * n_in]):
        acc = acc + _dot(a_ref[...], w_ref[...])
    o_ref[...] = acc


def _mm_res(a_list, w, res_parts, tm, tn):
    n = w.shape[1]
    tiles = _part_tiles(res_parts, tm)
    m = sum(tiles) * tm
    n_in = len(a_list)
    kb = a_list[0].shape[1]
    assert all(a.shape == (m, kb) for a in a_list) and w.shape[0] == n_in * kb
    w_list = [w] * n_in
    in_specs = [pl.BlockSpec((tm, kb), lambda i, j: (i, 0)) for _ in a_list]
    in_specs += [pl.BlockSpec((kb, tn), functools.partial(lambda i, j, r: (r, j), r=r))
                 for r in range(n_in)]
    in_specs += _part_specs(res_parts, tm, tn, lambda i, j: j)
    return pl.pallas_call(
        functools.partial(_mm_res_kernel, n_in=n_in, tiles=tiles),
        grid=(m // tm, n // tn),
        in_specs=in_specs,
        out_specs=pl.BlockSpec((tm, tn), lambda i, j: (i, j)),
        out_shape=jax.ShapeDtypeStruct((m, n), F32),
        compiler_params=_params("parallel", "arbitrary"),
        name="mm_residual",
    )(*a_list, *w_list, *res_parts)


HALO = 16


def _ffn_up_kernel(x_ref, xp_ref, xn_ref, g_ref, wg_ref, wu_ref, cw_ref, cb_ref, o_ref, n_ref,
                   *, tm, tiles_per_seq):
    i = pl.program_id(0)

    @pl.when(pl.program_id(1) == 0)
    def _():
        g = g_ref[...]
        first = (i % tiles_per_seq) == 0
        last = (i % tiles_per_seq) == tiles_per_seq - 1
        n_ref[0:HALO, :] = jnp.where(first, 0.0, _rms(xp_ref[...], g)).astype(BF16)
        n_ref[HALO:HALO + tm, :] = _rms(x_ref[...], g).astype(BF16)
        n_ref[HALO + tm:, :] = jnp.where(last, 0.0, _rms(xn_ref[...], g)).astype(BF16)

    ge = _dot(n_ref[...], wg_ref[...])
    rows = tm + 2 * HALO
    prev = pltpu.roll(ge, 1, 0)[HALO:HALO + tm, :]
    nxt = pltpu.roll(ge, rows - 1, 0)[HALO:HALO + tm, :]
    cw = cw_ref[...]
    gate = prev * cw[0:1, :] + ge[HALO:HALO + tm, :] * cw[1:2, :] + nxt * cw[2:3, :] + cb_ref[...]
    up = _dot(n_ref[HALO:HALO + tm, :], wu_ref[...])
    o_ref[...] = (gate * _sigmoid(gate) * up).astype(BF16)


def _ffn_up(h, g, w_up, conv_w, conv_b, seq):
    m, k = h.shape
    tm = _pick(seq, 1024, 512, 256, 128)
    tf = 512
    nf = D_FF // tf
    tps = seq // tm
    r = tm // HALO
    nblk = m // HALO
    return pl.pallas_call(
        functools.partial(_ffn_up_kernel, tm=tm, tiles_per_seq=tps),
        grid=(m // tm, nf),
        in_specs=[
            pl.BlockSpec((tm, k), lambda i, j: (i, 0)),
            pl.BlockSpec((HALO, k), lambda i, j: (jnp.maximum(i * r - 1, 0), 0)),
            pl.BlockSpec((HALO, k), lambda i, j: (jnp.minimum((i + 1) * r, nblk - 1), 0)),
            pl.BlockSpec((1, k), lambda i, j: (0, 0)),
            pl.BlockSpec((k, tf), lambda i, j: (0, j)),
            pl.BlockSpec((k, tf), lambda i, j: (0, j + nf)),
            pl.BlockSpec((3, tf), lambda i, j: (0, j)),
            pl.BlockSpec((1, tf), lambda i, j: (0, j)),
        ],
        out_specs=pl.BlockSpec((tm, tf), lambda i, j: (i, j)),
        out_shape=jax.ShapeDtypeStruct((m, D_FF), BF16),
        scratch_shapes=[pltpu.VMEM((tm + 2 * HALO, k), BF16)],
        compiler_params=_params("parallel", "arbitrary"),
        name="ffn_up",
    )(h, h, h, g.reshape(1, k), w_up, w_up, conv_w, conv_b.reshape(1, D_FF))


def _ple_kernel(h_ref, *refs, split, tn, p_tiles):
    p_refs = refs[:len(p_tiles)]
    g_ref, wg_ref, wp_ref, fn_ref = refs[len(p_tiles):len(p_tiles) + 4]
    out_refs = refs[len(p_tiles) + 4:]
    x = h_ref[...]
    xn = _rms(x, g_ref[...]).astype(BF16)
    pb = _part_read(p_refs, p_tiles, pl.program_id(0)).astype(BF16)
    y_ref = out_refs[-1]
    for c in range(D_MODEL // tn):
        sl = slice(c * tn, (c + 1) * tn)
        gate = _dot(xn, wg_ref[:, sl])
        y_ref[:, sl] = x[:, sl] + _dot(pb, wp_ref[:, sl]) * _sigmoid(gate)
    if split is not None:
        first_ref, second_ref, _ = out_refs
        y = _rms(y_ref[...], fn_ref[...])
        i = pl.program_id(0)

        @pl.when(i < split)
        def _():
            first_ref[...] = y

        @pl.when(i >= split)
        def _():
            second_ref[...] = y


def _ple(h, p_parts, layer, g, wg, wp, fn, m_first=None):
    m, k = h.shape
    tm = _pick(math.gcd(*[p.shape[0] // DEPTH for p in p_parts]), 512, 256, 128)
    p_tiles = tuple(p.shape[0] // DEPTH // tm for p in p_parts)
    assert sum(p_tiles) * tm == m and (m_first is None or m_first % tm == 0)
    p_specs, start = [], 0
    for nt in p_tiles:
        p_specs.append(pl.BlockSpec((tm, PLE_DIM), functools.partial(
            lambda i, s, nt: (layer * nt + jnp.clip(i - s, 0, nt - 1), 0), s=start, nt=nt)))
        start += nt
    tile = pl.BlockSpec((tm, k), lambda i: (i, 0))
    if m_first is None:
        split, scratch = None, []
        out_specs, out_shape = tile, jax.ShapeDtypeStruct((m, k), F32)
    else:
        split, scratch = m_first // tm, [pltpu.VMEM((tm, k), F32)]
        out_specs = [pl.BlockSpec((tm, k), lambda i: (jnp.minimum(i, split - 1), 0)),
                     pl.BlockSpec((tm, k), lambda i: (jnp.maximum(i - split, 0), 0))]
        out_shape = [jax.ShapeDtypeStruct((m_first, k), F32), jax.ShapeDtypeStruct((m - m_first, k), F32)]
    return pl.pallas_call(
        functools.partial(_ple_kernel, split=split, tn=512, p_tiles=p_tiles),
        grid=(m // tm,),
        in_specs=[tile] + p_specs + [
            pl.BlockSpec((1, k), lambda i: (0, 0)),
            pl.BlockSpec(wg.shape, lambda i: (0, 0)),
            pl.BlockSpec(wp.shape, lambda i: (0, 0)),
            pl.BlockSpec((1, k), lambda i: (0, 0)),
        ],
        out_specs=out_specs,
        out_shape=out_shape,
        scratch_shapes=scratch,
        compiler_params=_params("arbitrary"),
        name="ple_gate",
    )(h, *p_parts, g.reshape(1, k), wg, wp, fn.reshape(1, k))


def _rot_cols(w):
    half = ROPE_DIM // 2
    return jnp.concatenate([-w[..., half:], w[..., :half]], axis=-1)


def _pad_rope(w):
    return jnp.concatenate([w, jnp.zeros_like(w)], axis=-1)


def _prepare(attn_norm, w_in, hg_lower, hg_norm, w_q_b, w_kv_b, w_out, w_up, w_down,
             w_ple_gate, w_ple_proj):
    w_kr = w_in[:, :, OFF_KR:]
    wq = w_q_b.reshape(DEPTH, LORA, HEADS, HEAD_DIM + ROPE_DIM)
    wq_r = wq[..., HEAD_DIM:]
    wq = jnp.concatenate([wq[..., :HEAD_DIM], _pad_rope(wq_r), _pad_rope(_rot_cols(wq_r))], axis=-1)
    wkv = w_kv_b.reshape(DEPTH, LORA, HEADS, 2 * HEAD_DIM)

    lb = jnp.cumsum(jax.nn.softmax(hg_lower.astype(F32), axis=1), axis=1)
    lb = jnp.maximum(lb - lb[:, :1], 0.0)
    loglb = jnp.maximum(jnp.log(lb) * LOG2E, NEG).reshape(2, DEPTH, HEADS, HEAD_DIM)
    log1m = (jnp.log1p(-lb) * LOG2E).reshape(2, DEPTH, HEADS, HEAD_DIM)
    gain = hg_norm.astype(F32).reshape(DEPTH, HEADS, HEAD_DIM)
    zeros = jnp.zeros_like(gain)
    crow = jnp.stack([loglb[0], log1m[0], loglb[1], log1m[1], gain, zeros, zeros, zeros], axis=2)

    return dict(
        w_in=w_in[:, :, :OFF_KR].astype(BF16),
        w_kr=jnp.concatenate([_pad_rope(w_kr), _pad_rope(_rot_cols(w_kr))], axis=-1).astype(BF16),
        w_q=wq.reshape(DEPTH, LORA, HEADS * 3 * LANES).astype(BF16),
        w_k=wkv[..., :HEAD_DIM].reshape(DEPTH, LORA, HG_WIDTH).astype(BF16),
        w_vt=jnp.swapaxes(wkv[..., HEAD_DIM:].reshape(DEPTH, LORA, HG_WIDTH), 1, 2).astype(BF16),
        w_out=w_out.astype(BF16),
        w_up=w_up.astype(BF16),
        w_down=w_down.astype(BF16),
        w_ple_gate=w_ple_gate.astype(BF16),
        w_ple_proj=w_ple_proj.astype(BF16),
        crow=crow,
    )


def _rope_tables(seq):
    inv = ROPE_THETA ** (-jnp.arange(0, ROPE_DIM, 2, dtype=F32) / ROPE_DIM)
    ang = jnp.arange(seq, dtype=F32)[:, None] * inv[None, :]
    pad = jnp.zeros((seq, LANES - ROPE_DIM), F32)
    cosz = jnp.concatenate([jnp.cos(ang), jnp.cos(ang), pad], axis=-1)
    sinz = jnp.concatenate([jnp.sin(ang), jnp.sin(ang), pad], axis=-1)
    return cosz, sinz


def _trunk(xs, ps, attn_norm, q_a_norm, kv_a_norm, ffn_norm, conv_w, conv_b, ple_norm, final_norm, prm):
    seq, d = xs[0].shape[1:]
    nbs = [x.shape[0] for x in xs]
    nb = sum(nbs)
    h_parts = [x.reshape(n * seq, d) for x, n in zip(xs, nbs)]
    p_parts = [p.reshape(DEPTH * n * seq, PLE_DIM) for p, n in zip(ps, nbs)]
    cosz, sinz = _rope_tables(seq)
    tm = _pick(seq, 1024, 512, 256, 128)
    tk = _pick(seq, 1024, 512, 256, 128)
    for l in range(DEPTH):
        z, zkr = _norm_mm(h_parts, attn_norm[l], prm["w_in"][l], prm["w_kr"][l], tm,
                          1024 // len(h_parts))
        o_hg = _hgrn(z, prm["crow"][l], nb, seq)
        qp = _qproj(z, q_a_norm[l], prm["w_q"][l], cosz, sinz, seq)
        kp, vt = _kvproj(z, zkr, kv_a_norm[l], prm["w_k"][l], prm["w_vt"][l], cosz, sinz, nb, seq, tk)
        o_mla = _attention(qp, kp, vt, nb, seq, tk)
        h = _mm_res([o_hg, o_mla], prm["w_out"][l], h_parts, tm, 512)
        act = _ffn_up(h, ffn_norm[l], prm["w_up"][l], conv_w[l], conv_b[l], seq)
        h = _mm_res([act], prm["w_down"][l], [h], tm, 512)
        h = _ple(h, p_parts, l, ple_norm[l], prm["w_ple_gate"][l], prm["w_ple_proj"][l], final_norm,
                 m_first=nbs[0] * seq if l == DEPTH - 1 else None)
        h_parts = [h]
    return h[0].reshape(nbs[0], seq, d), h[1].reshape(nbs[1], seq, d)


def kernel(x_prompt, x_sample, p_prompt, p_sample, attn_norm, w_in, hg_lower, hg_norm, q_a_norm,
           w_q_b, kv_a_norm, w_kv_b, w_out, ffn_norm, w_up, conv_w, conv_b, w_down, ple_norm,
           w_ple_gate, w_ple_proj, final_norm):
    assert x_prompt.shape[1:] == x_sample.shape[1:], "both groups must share the sequence length"
    prm = _prepare(attn_norm, w_in, hg_lower, hg_norm, w_q_b, w_kv_b, w_out, w_up, w_down,
                   w_ple_gate, w_ple_proj)
    return _trunk([x_prompt, x_sample], [p_prompt, p_sample], attn_norm, q_a_norm, kv_a_norm,
                  ffn_norm, conv_w, conv_b, ple_norm, final_norm, prm)
```

```python
import functools
import math

import jax
import jax.numpy as jnp
import numpy as np
from jax import lax
from jax.experimental import pallas as pl
from jax.experimental.pallas import tpu as pltpu

F32 = jnp.float32
BF16 = jnp.bfloat16

D_MODEL = 2048
DEPTH = 4
HEADS = 8
HEAD_DIM = 128
ROPE_DIM = 64
LORA = 512
HG_WIDTH = HEADS * HEAD_DIM
OFF_Q, OFF_FF, OFF_FB, OFF_I, OFF_G, OFF_QA, OFF_KVA, OFF_KR = (
    0, 1024, 2048, 3072, 4096, 5120, 5632, 6144)
D_FF = 5632
PLE_DIM = 256
EPS = 1e-6
ROPE_THETA = 10000.0
NEG = -1e30

LANES = 128
SUBLANES = 8
VMEM_LIMIT = 56 * 2**20

HG_CHUNK = 64
HG_ROWS = SUBLANES
HG_STEP = 512
HG_UNROLL = 8
Q_PAD = 2 * LANES
LOG2E = math.log2(math.e)
QK_SCALE = (HEAD_DIM + ROPE_DIM) ** -0.5 * LOG2E
ATT_GROUP = 2 * LANES
ATT_BLOCKS = 2
VT_ROWS = HEAD_DIM + 16
MIX_ROUNDS = 2


def _params(*sem):
    return pltpu.CompilerParams(dimension_semantics=sem, vmem_limit_bytes=VMEM_LIMIT)


def _pick(n, *cands):
    for c in cands:
        if n % c == 0:
            return c
    raise ValueError(f"no tile for {n} in {cands}")


def _rms(x, g):
    ms = jnp.mean(x * x, axis=-1, keepdims=True)
    return x * lax.rsqrt(ms + EPS) * g


def _sigmoid(x):
    return 1.0 / (1.0 + jnp.exp(-x))


def _dot(a, b):
    return jnp.dot(a, b, preferred_element_type=F32)


def _dot_nt(a, b):
    return lax.dot_general(a, b, (((1,), (1,)), ((), ())), preferred_element_type=F32)


def _dot_tn(a, b):
    return lax.dot_general(a, b, (((0,), (0,)), ((), ())), preferred_element_type=F32)


def _part_tiles(parts, tm):
    assert all(p.shape[0] % tm == 0 for p in parts)
    return tuple(p.shape[0] // tm for p in parts)


def _part_specs(parts, tm, width, col):
    specs, start = [], 0
    for nt in _part_tiles(parts, tm):
        def imap(i, *rest, start=start, nt=nt):
            inside = (i >= start) & (i < start + nt)
            return jnp.clip(i - start, 0, nt - 1), jnp.where(inside, col(i, *rest), 0)
        specs.append(pl.BlockSpec((tm, width), imap))
        start += nt
    return specs


def _part_read(refs, tiles, i):
    x, start = refs[0][...], tiles[0]
    for ref, nt in zip(refs[1:], tiles[1:]):
        x = jnp.where(i >= start, ref[...], x)
        start += nt
    return x


def _norm_mm_kernel(*refs, tiles):
    x_refs = refs[:len(tiles)]
    g_ref, w_ref, ws_ref, o_ref, os_ref, xn_ref = refs[len(tiles):]

    @pl.when(pl.program_id(1) == 0)
    def _():
        x = _part_read(x_refs, tiles, pl.program_id(0))
        xn_ref[...] = _rms(x, g_ref[...]).astype(BF16)
        os_ref[...] = _dot(xn_ref[...], ws_ref[...])

    o_ref[...] = _dot(xn_ref[...], w_ref[...])


def _norm_mm(x_parts, g, w, w_side, tm, tn):
    k, n = w.shape
    ns = w_side.shape[1]
    tiles = _part_tiles(x_parts, tm)
    m = sum(tiles) * tm
    return pl.pallas_call(
        functools.partial(_norm_mm_kernel, tiles=tiles),
        grid=(m // tm, n // tn),
        in_specs=_part_specs(x_parts, tm, k, lambda i, j: 0) + [
            pl.BlockSpec((1, k), lambda i, j: (0, 0)),
            pl.BlockSpec((k, tn), lambda i, j: (0, j)),
            pl.BlockSpec((k, ns), lambda i, j: (0, 0)),
        ],
        out_specs=[pl.BlockSpec((tm, tn), lambda i, j: (i, j)),
                   pl.BlockSpec((tm, ns), lambda i, j: (i, 0))],
        out_shape=[jax.ShapeDtypeStruct((m, n), F32), jax.ShapeDtypeStruct((m, ns), F32)],
        scratch_shapes=[pltpu.VMEM((tm, k), BF16)],
        compiler_params=_params("parallel", "arbitrary"),
        name="norm_mm",
    )(*x_parts, g.reshape(1, k), w, w_side)


def _hgrn_gates(z, loglb, log1m):
    z2 = z * LOG2E
    l1p = jnp.log2(1.0 + jnp.exp2(-jnp.abs(z2)))
    c = log1m + (jnp.minimum(z2, 0.0) - l1p)
    logf = jnp.maximum(loglb, c) + jnp.log2(1.0 + jnp.exp2(-jnp.abs(loglb - c)))
    return logf, c - z2


def _hgrn_chunk(q, zf, v, cmat, lvl, emat, loglb, log1m, st_ref, cb_ref, reverse):
    c = q.shape[0]
    nv = c // HG_ROWS
    logf, logk = _hgrn_gates(zf, loglb, log1m)

    hi = logf.astype(BF16)
    lo = (logf - hi.astype(F32)).astype(BF16)
    yield
    bb = _dot(cmat, jnp.concatenate([hi, lo], axis=1))
    b = bb[:, :LANES] + bb[:, LANES:]

    def rows(x, p):
        i = nv - 1 - p if reverse else p
        return x[i * HG_ROWS:(i + 1) * HG_ROWS, :]

    def assemble(groups):
        return jnp.concatenate(groups[::-1] if reverse else groups, axis=0)

    edge = 0 if reverse else HG_ROWS - 1
    b_p = [rows(b, p) for p in range(nv)]
    q_p = [rows(q, p) for p in range(nv)]
    lk_p = [rows(logk, p) for p in range(nv)]
    bnd = [jnp.broadcast_to(x[edge:edge + 1, :], (HG_ROWS, LANES)) for x in b_p]
    zero = jnp.zeros((HG_ROWS, LANES), F32)

    a = jnp.zeros((c, c), F32)
    level, mv = 1, nv // 2
    while mv >= 1:
        qm = [zero] * nv
        km = [zero] * nv
        for g in range(nv // (2 * mv)):
            ref = bnd[2 * mv * g + mv - 1]
            for p in range(2 * mv * g, 2 * mv * g + mv):
                km[p] = jnp.exp2(lk_p[p] + (ref - b_p[p]))
            for p in range(2 * mv * g + mv, 2 * mv * (g + 1)):
                qm[p] = q_p[p] * jnp.exp2(b_p[p] - ref)
        yield
        al = _dot_nt(assemble(qm).astype(BF16), assemble(km).astype(BF16))
        a = jnp.where(lvl == float(level), al, a)
        level += 1
        mv //= 2

    cb_ref[...] = b - logk
    slabs = []
    for s in range(HG_ROWS):
        col = []
        for p in range(nv):
            r = (nv - 1 - p if reverse else p) * HG_ROWS + s
            cs = jnp.broadcast_to(cb_ref[r:r + 1, :], (HG_ROWS, LANES))
            col.append(q_p[p] * jnp.exp2(jnp.minimum(b_p[p] - cs, 0.0)))
        slabs.append(assemble(col).astype(BF16))
        yield
    adiag = _dot(jnp.concatenate(slabs, axis=1), emat)
    a = jnp.where(lvl == float(level), adiag, a)

    tot = bnd[nv - 1]
    qbar = assemble([q_p[p] * jnp.exp2(b_p[p]) for p in range(nv)]).astype(BF16)
    kbar = assemble([jnp.exp2(lk_p[p] + (tot - b_p[p])) for p in range(nv)]).astype(BF16)
    vb = v.astype(BF16)
    yield
    st = st_ref[...]
    o = _dot(a.astype(BF16), vb) + _dot_nt(qbar, st.astype(BF16))
    st_ref[...] = jnp.exp2(tot[0:1, :]) * st + _dot_tn(vb, kbar)
    return o


def _lockstep(gens):
    results = [None] * len(gens)
    live = list(range(len(gens)))
    while live:
        for idx in list(live):
            try:
                next(gens[idx])
            except StopIteration as done:
                results[idx] = done.value
                live.remove(idx)
    return results


def _hgrn_kernel(qf_ref, zf_ref, vf_ref, gf_ref, qb_ref, zb_ref, vb_ref, gb_ref,
                 crow_ref, cmat_ref, lvl_ref, emat_ref, o_ref,
                 part_ref, stf_ref, stb_ref, cb_ref, *, step, chunk, nsteps):
    c = pl.program_id(2)

    @pl.when(c == 0)
    def _():
        stf_ref[...] = jnp.zeros_like(stf_ref)
        stb_ref[...] = jnp.zeros_like(stb_ref)
        part_ref[...] = jnp.zeros_like(part_ref)

    crow = crow_ref[0]
    gain = crow[4:5, :]
    emat = emat_ref[...]
    nch = step // chunk

    def emit(o, row, g):
        t = o + part_ref[pl.ds(row, chunk), :]
        part_ref[pl.ds(row, chunk), :] = t
        o_ref[pl.ds(row, chunk), :] = (_rms(t, gain) * (g * _sigmoid(g))).astype(BF16)

    def fwd(j, u):
        rf = pl.multiple_of(j * chunk, chunk)
        sl = pl.ds(rf, chunk)
        gen = _hgrn_chunk(qf_ref[sl, :], zf_ref[sl, :], vf_ref[sl, :], cmat_ref[0], lvl_ref[0], emat,
                          crow[0:1, :], crow[1:2, :], stf_ref, cb_ref.at[0, u], False)
        return gen, pl.multiple_of(c * step + rf, chunk), gf_ref, sl

    def bwd(j, u):
        rb = pl.multiple_of((nch - 1 - j) * chunk, chunk)
        sl = pl.ds(rb, chunk)
        gen = _hgrn_chunk(qb_ref[sl, :], zb_ref[sl, :], vb_ref[sl, :], cmat_ref[1], lvl_ref[1], emat,
                          crow[2:3, :], crow[3:4, :], stb_ref, cb_ref.at[1, u], True)
        return gen, pl.multiple_of((nsteps - 1 - c) * step + rb, chunk), gb_ref, sl

    unroll = math.gcd(HG_UNROLL, nch)

    def body(i, carry):
        chains = [f(i * unroll + u, u) for u in range(unroll) for f in (fwd, bwd)]
        outs = _lockstep([ch[0] for ch in chains])
        for o, (_, row, g_ref, sl) in zip(outs, chains):
            emit(o, row, g_ref[sl, :])
        return carry

    lax.fori_loop(0, nch // unroll, body, 0)


def _hgrn_consts(chunk):
    t = np.arange(chunk)[:, None]
    s = np.arange(chunk)[None, :]
    cm = np.stack([(s <= t), (s >= t)]).astype(np.float32)
    nlev = int(round(math.log2(chunk // HG_ROWS)))
    lv = np.zeros((chunk, chunk), np.float32)
    for level in range(1, nlev + 1):
        m = chunk >> level
        hit = (t // (2 * m) == s // (2 * m)) & (t // m != s // m) & (s < t)
        lv[hit] = level
    lv[((t // HG_ROWS) == (s // HG_ROWS)) & (s <= t)] = nlev + 1
    lvl = np.stack([lv, lv.T]).astype(np.float32)
    em = (np.arange(HG_ROWS * LANES)[:, None] // LANES == (np.arange(chunk)[None, :] % HG_ROWS))
    return jnp.asarray(cm, BF16), jnp.asarray(lvl, F32), jnp.asarray(em.astype(np.float32), BF16)


def _hgrn(z, crow, nb, seq):
    m = z.shape[0]
    step = _pick(seq // 2, HG_STEP, 256, 128, 64)
    chunk = HG_CHUNK
    n = seq // step
    half = n // 2
    cmat, lvl, emat = _hgrn_consts(chunk)

    def zspec(col, fwd, gate=False):
        if fwd:
            blk = (lambda c: jnp.maximum(c, half)) if gate else (lambda c: c)
        else:
            blk = (lambda c: jnp.minimum(n - 1 - c, half - 1)) if gate else (lambda c: n - 1 - c)
        return pl.BlockSpec((step, LANES), lambda b, h, c: (b * n + blk(c), col // LANES + h))

    const3 = lambda shape: pl.BlockSpec(shape, lambda b, h, c: (0, 0, 0))
    return pl.pallas_call(
        functools.partial(_hgrn_kernel, step=step, chunk=chunk, nsteps=n),
        grid=(nb, HEADS, n),
        in_specs=[
            zspec(OFF_Q, True), zspec(OFF_FF, True), zspec(OFF_I, True), zspec(OFF_G, True, True),
            zspec(OFF_Q, False), zspec(OFF_FB, False), zspec(OFF_I, False), zspec(OFF_G, False, True),
            pl.BlockSpec((1, SUBLANES, LANES), lambda b, h, c: (h, 0, 0)),
            const3(cmat.shape), const3(lvl.shape),
            pl.BlockSpec(emat.shape, lambda b, h, c: (0, 0)),
        ],
        out_specs=pl.BlockSpec((seq, LANES), lambda b, h, c: (b, h)),
        out_shape=jax.ShapeDtypeStruct((m, HG_WIDTH), BF16),
        scratch_shapes=[
            pltpu.VMEM((seq, LANES), F32),
            pltpu.VMEM((HEAD_DIM, HEAD_DIM), F32),
            pltpu.VMEM((HEAD_DIM, HEAD_DIM), F32),
            pltpu.VMEM((2, HG_UNROLL, chunk, LANES), F32),
        ],
        compiler_params=_params("parallel", "parallel", "arbitrary"),
        name="hgrn2",
    )(z, z, z, z, z, z, z, z, crow, cmat, lvl, emat)


def _qproj_kernel(z_ref, g_ref, w_ref, cos_ref, sin_ref, o_ref):
    xn = _rms(z_ref[...], g_ref[...]).astype(BF16)
    cz = cos_ref[...]
    sz = sin_ref[...]
    for h in range(HEADS):
        acc = _dot(xn, w_ref[:, h * 3 * LANES:(h + 1) * 3 * LANES])
        o_ref[:, h * Q_PAD:h * Q_PAD + LANES] = (acc[:, :LANES] * QK_SCALE).astype(BF16)
        rope = acc[:, LANES:2 * LANES] * cz + acc[:, 2 * LANES:] * sz
        o_ref[:, h * Q_PAD + LANES:(h + 1) * Q_PAD] = (rope * QK_SCALE).astype(BF16)


def _qproj(z, g, w, cosz, sinz, seq):
    m = z.shape[0]
    tm = _pick(seq, 512, 256, 128)
    ns = seq // tm
    return pl.pallas_call(
        _qproj_kernel,
        grid=(m // tm,),
        in_specs=[
            pl.BlockSpec((tm, LORA), lambda i: (i, OFF_QA // LORA)),
            pl.BlockSpec((1, LORA), lambda i: (0, 0)),
            pl.BlockSpec(w.shape, lambda i: (0, 0)),
            pl.BlockSpec((tm, LANES), lambda i: (i % ns, 0)),
            pl.BlockSpec((tm, LANES), lambda i: (i % ns, 0)),
        ],
        out_specs=pl.BlockSpec((tm, HEADS * Q_PAD), lambda i: (i, 0)),
        out_shape=jax.ShapeDtypeStruct((m, HEADS * Q_PAD), BF16),
        compiler_params=_params("parallel"),
        name="mla_q_proj",
    )(z, g.reshape(1, LORA), w, cosz, sinz)


def _kvproj_kernel(z_ref, kr_ref, g_ref, wk_ref, wvt_ref, cos_ref, sin_ref, k_ref, vt_ref):
    xn = _rms(z_ref[...], g_ref[...]).astype(BF16)
    kr = kr_ref[...]
    krope = (kr[:, :LANES] * cos_ref[...] + kr[:, LANES:] * sin_ref[...]).astype(BF16)
    kn = _dot(xn, wk_ref[...])
    for h in range(HEADS):
        k_ref[:, h * Q_PAD:h * Q_PAD + LANES] = kn[:, h * LANES:(h + 1) * LANES].astype(BF16)
        k_ref[:, h * Q_PAD + LANES:(h + 1) * Q_PAD] = krope
    vt = _dot_nt(wvt_ref[...], xn).astype(BF16)
    ones = jnp.ones((VT_ROWS - HEAD_DIM, vt.shape[1]), BF16)
    for h in range(HEADS):
        vt_ref[0, 0, h, :HEAD_DIM, :] = vt[h * HEAD_DIM:(h + 1) * HEAD_DIM, :]
        vt_ref[0, 0, h, HEAD_DIM:, :] = ones


def _kvproj(z, zkr, g, wk, wvt, cosz, sinz, nb, seq, tk):
    m = z.shape[0]
    ns = seq // tk
    return pl.pallas_call(
        _kvproj_kernel,
        grid=(m // tk,),
        in_specs=[
            pl.BlockSpec((tk, LORA), lambda i: (i, OFF_KVA // LORA)),
            pl.BlockSpec((tk, 2 * LANES), lambda i: (i, 0)),
            pl.BlockSpec((1, LORA), lambda i: (0, 0)),
            pl.BlockSpec(wk.shape, lambda i: (0, 0)),
            pl.BlockSpec(wvt.shape, lambda i: (0, 0)),
            pl.BlockSpec((tk, LANES), lambda i: (i % ns, 0)),
            pl.BlockSpec((tk, LANES), lambda i: (i % ns, 0)),
        ],
        out_specs=[
            pl.BlockSpec((tk, HEADS * Q_PAD), lambda i: (i, 0)),
            pl.BlockSpec((1, 1, HEADS, VT_ROWS, tk), lambda i: (i // ns, i % ns, 0, 0, 0)),
        ],
        out_shape=[
            jax.ShapeDtypeStruct((m, HEADS * Q_PAD), BF16),
            jax.ShapeDtypeStruct((nb, ns, HEADS, VT_ROWS, tk), BF16),
        ],
        compiler_params=_params("parallel"),
        name="mla_kv_proj",
    )(z, zkr, g.reshape(1, LORA), wk, wvt, cosz, sinz)


def _col_reduce(x, op, slab=64):
    r, c = x.shape
    if r > slab and r % slab == 0:
        x = op(x.reshape(r // slab, slab, c), axis=0)
    return op(x, axis=0, keepdims=True)


def _attn_kernel(q_ref, k_ref, vt_ref, o_ref, acc_ref, st_ref, mx_ref, *, tk, nk, nblk):
    tq = q_ref.shape[0]
    nslot, ng, _, gw = st_ref.shape
    assert nblk % nslot == 0 and nk % nblk == 0
    acc_ref[...] = jnp.zeros_like(acc_ref)

    def scores(j, slot, g):
        kb = k_ref[pl.ds(pl.multiple_of(j * tk, tk), tk), :]
        st = _dot_nt(kb, q_ref[g * gw:(g + 1) * gw, :])
        st_ref[slot, g] = st
        mx_ref[slot, g] = _col_reduce(st, jnp.max)

    def consume(j, slot, g, m):
        cols = slice(g * gw, (g + 1) * gw)
        m_new = jnp.maximum(m, mx_ref[slot, g])
        alpha = jnp.exp2(m - m_new)
        p = jnp.exp2((st_ref[slot, g] - m_new).astype(BF16))
        acc_ref[:, cols] = alpha * acc_ref[:, cols] + _dot(vt_ref[0, j, 0], p)
        return m_new

    def trip(j, ms, last):
        ms = list(ms)
        for u in range(nblk):
            for g in range(ng):
                if not (last and u == nblk - 1):
                    scores(j + u + 1, (u + 1) % nslot, g)
                ms[g] = consume(j + u, u % nslot, g, ms[g])
        return tuple(ms)

    for g in range(ng):
        scores(0, 0, g)
    init = tuple(jnp.full((1, gw), NEG, F32) for _ in range(ng))
    ms = lax.fori_loop(0, nk // nblk - 1, lambda i, ms: trip(nblk * i, ms, False), init)
    trip(nk - nblk, ms, True)
    for g in range(ng):
        cols = slice(g * gw, (g + 1) * gw)
        o_ref[cols, :] = (acc_ref[:HEAD_DIM, cols] / acc_ref[HEAD_DIM:HEAD_DIM + 1, cols]).T.astype(BF16)


def _attention(qp, kp, vt, nb, seq, tk):
    m = qp.shape[0]
    tq = _pick(seq, 1024, 512, 256, 128)
    nq = seq // tq
    nk = seq // tk
    nblk = ATT_BLOCKS if nk % ATT_BLOCKS == 0 else 2
    assert nk % nblk == 0, "the key loop handles an even number of blocks per trip"
    gw = min(ATT_GROUP, tq)
    return pl.pallas_call(
        functools.partial(_attn_kernel, tk=tk, nk=nk, nblk=nblk),
        grid=(nb, HEADS, nq),
        in_specs=[
            pl.BlockSpec((tq, Q_PAD), lambda b, h, i: (b * nq + i, h)),
            pl.BlockSpec((seq, Q_PAD), lambda b, h, i: (b, h)),
            pl.BlockSpec((1, nk, 1, VT_ROWS, tk), lambda b, h, i: (b, 0, h, 0, 0)),
        ],
        out_specs=pl.BlockSpec((tq, HEAD_DIM), lambda b, h, i: (b * nq + i, h)),
        out_shape=jax.ShapeDtypeStruct((m, HG_WIDTH), BF16),
        scratch_shapes=[pltpu.VMEM((VT_ROWS, tq), F32), pltpu.VMEM((2, tq // gw, tk, gw), F32),
                        pltpu.VMEM((2, tq // gw, 1, gw), F32)],
        compiler_params=_params("parallel", "parallel", "arbitrary"),
        name="mla_attention",
    )(qp, kp, vt)


def _interleave(units, gens, rounds):
    results = [None] * len(gens)
    live = list(range(len(gens)))

    def advance():
        for idx in list(live):
            try:
                next(gens[idx])
            except StopIteration as done:
                results[idx] = done.value
                live.remove(idx)

    for unit in units:
        unit()
        for _ in range(rounds):
            advance()
    while live:
        advance()
    return results


def _mixer_kernel(qf_ref, zf_ref, vf_ref, gf_ref, qb_ref, zb_ref, vb_ref, gb_ref,
                  crow_ref, cmat_ref, lvl_ref, emat_ref, q_ref, k_ref, vt_ref,
                  ohg_ref, oat_ref,
                  part_ref, stf_ref, stb_ref, cb_ref, acc_ref, st_ref, mx_ref,
                  *, chunk, tk, nk, nblk):
    c = pl.program_id(2)
    nsteps = pl.num_programs(2)
    tq = q_ref.shape[0]
    nslot, ng, _, gw = st_ref.shape
    ntrips = nk // nblk
    nch = tq // chunk
    assert nblk % nslot == 0 and nk % nblk == 0 and nch % ntrips == 0
    hpt = nch // ntrips

    @pl.when(c == 0)
    def _():
        stf_ref[...] = jnp.zeros_like(stf_ref)
        stb_ref[...] = jnp.zeros_like(stb_ref)
        part_ref[...] = jnp.zeros_like(part_ref)

    acc_ref[...] = jnp.zeros_like(acc_ref)
    crow = crow_ref[0]
    gain = crow[4:5, :]
    emat = emat_ref[...]

    def emit(o, row, g):
        t = o + part_ref[pl.ds(row, chunk), :]
        part_ref[pl.ds(row, chunk), :] = t
        ohg_ref[pl.ds(row, chunk), :] = (_rms(t, gain) * (g * _sigmoid(g))).astype(BF16)

    def fwd(j, u):
        rf = pl.multiple_of(j * chunk, chunk)
        sl = pl.ds(rf, chunk)
        gen = _hgrn_chunk(qf_ref[sl, :], zf_ref[sl, :], vf_ref[sl, :], cmat_ref[0], lvl_ref[0], emat,
                          crow[0:1, :], crow[1:2, :], stf_ref, cb_ref.at[0, u], False)
        return gen, pl.multiple_of(c * tq + rf, chunk), gf_ref, sl

    def bwd(j, u):
        rb = pl.multiple_of((nch - 1 - j) * chunk, chunk)
        sl = pl.ds(rb, chunk)
        gen = _hgrn_chunk(qb_ref[sl, :], zb_ref[sl, :], vb_ref[sl, :], cmat_ref[1], lvl_ref[1], emat,
                          crow[2:3, :], crow[3:4, :], stb_ref, cb_ref.at[1, u], True)
        return gen, pl.multiple_of((nsteps - 1 - c) * tq + rb, chunk), gb_ref, sl

    def scores(j, slot, g):
        kb = k_ref[pl.ds(pl.multiple_of(j * tk, tk), tk), :]
        st = _dot_nt(kb, q_ref[g * gw:(g + 1) * gw, :])
        st_ref[slot, g] = st
        mx_ref[slot, g] = _col_reduce(st, jnp.max)

    def consume(j, slot, g, m):
        cols = slice(g * gw, (g + 1) * gw)
        m_new = jnp.maximum(m, mx_ref[slot, g])
        alpha = jnp.exp2(m - m_new)
        p = jnp.exp2((st_ref[slot, g] - m_new).astype(BF16))
        acc_ref[:, cols] = alpha * acc_ref[:, cols] + _dot(vt_ref[0, j, 0], p)
        return m_new

    def trip(t, ms, last):
        ms = list(ms)
        j = nblk * t
        units = []
        for u in range(nblk):
            for g in range(ng):
                def unit(u=u, g=g):
                    if not (last and u == nblk - 1):
                        scores(j + u + 1, (u + 1) % nslot, g)
                    ms[g] = consume(j + u, u % nslot, g, ms[g])
                units.append(unit)
        chains = [f(t * hpt + v, v) for v in range(hpt) for f in (fwd, bwd)]
        outs = _interleave(units, [ch[0] for ch in chains], MIX_ROUNDS)
        for o, (_, row, g_ref, sl) in zip(outs, chains):
            emit(o, row, g_ref[sl, :])
        return tuple(ms)

    for g in range(ng):
        scores(0, 0, g)
    init = tuple(jnp.full((1, gw), NEG, F32) for _ in range(ng))
    ms = lax.fori_loop(0, ntrips - 1, lambda t, ms: trip(t, ms, False), init)
    trip(ntrips - 1, ms, True)
    for g in range(ng):
        cols = slice(g * gw, (g + 1) * gw)
        oat_ref[cols, :] = (acc_ref[:HEAD_DIM, cols] / acc_ref[HEAD_DIM:HEAD_DIM + 1, cols]).T.astype(BF16)


def _mixer(z, crow, qp, kp, vt, nb, seq, tk):
    m = z.shape[0]
    tq = _pick(seq, 1024, 512, 256, 128)
    n = seq // tq
    assert n % 2 == 0, "the two scan directions meet in the middle"
    half = n // 2
    nk = seq // tk
    nblk = ATT_BLOCKS if nk % ATT_BLOCKS == 0 else 2
    gw = min(ATT_GROUP, tq)
    chunk = HG_CHUNK
    hpt = tq // chunk // (nk // nblk)
    cmat, lvl, emat = _hgrn_consts(chunk)

    def zspec(col, fwd, gate=False):
        if fwd:
            blk = (lambda c: jnp.maximum(c, half)) if gate else (lambda c: c)
        else:
            blk = (lambda c: jnp.minimum(n - 1 - c, half - 1)) if gate else (lambda c: n - 1 - c)
        return pl.BlockSpec((tq, LANES), lambda b, h, c: (b * n + blk(c), col // LANES + h))

    const3 = lambda shape: pl.BlockSpec(shape, lambda b, h, c: (0, 0, 0))
    return pl.pallas_call(
        functools.partial(_mixer_kernel, chunk=chunk, tk=tk, nk=nk, nblk=nblk),
        grid=(nb, HEADS, n),
        in_specs=[
            zspec(OFF_Q, True), zspec(OFF_FF, True), zspec(OFF_I, True), zspec(OFF_G, True, True),
            zspec(OFF_Q, False), zspec(OFF_FB, False), zspec(OFF_I, False), zspec(OFF_G, False, True),
            pl.BlockSpec((1, SUBLANES, LANES), lambda b, h, c: (h, 0, 0)),
            const3(cmat.shape), const3(lvl.shape),
            pl.BlockSpec(emat.shape, lambda b, h, c: (0, 0)),
            pl.BlockSpec((tq, Q_PAD), lambda b, h, c: (b * n + c, h)),
            pl.BlockSpec((seq, Q_PAD), lambda b, h, c: (b, h)),
            pl.BlockSpec((1, nk, 1, VT_ROWS, tk), lambda b, h, c: (b, 0, h, 0, 0)),
        ],
        out_specs=[pl.BlockSpec((seq, LANES), lambda b, h, c: (b, h)),
                   pl.BlockSpec((tq, HEAD_DIM), lambda b, h, c: (b * n + c, h))],
        out_shape=[jax.ShapeDtypeStruct((m, HG_WIDTH), BF16), jax.ShapeDtypeStruct((m, HG_WIDTH), BF16)],
        scratch_shapes=[
            pltpu.VMEM((seq, LANES), F32),
            pltpu.VMEM((HEAD_DIM, HEAD_DIM), F32),
            pltpu.VMEM((HEAD_DIM, HEAD_DIM), F32),
            pltpu.VMEM((2, hpt, chunk, LANES), F32),
            pltpu.VMEM((VT_ROWS, tq), F32),
            pltpu.VMEM((2, tq // gw, tk, gw), F32),
            pltpu.VMEM((2, tq // gw, 1, gw), F32),
        ],
        compiler_params=_params("parallel", "parallel", "arbitrary"),
        name="token_mixer",
    )(z, z, z, z, z, z, z, z, crow, cmat, lvl, emat, qp, kp, vt)


def _mm_res_kernel(*refs, n_in, tiles):
    res_refs, o_ref = refs[2 * n_in:-1], refs[-1]
    acc = _part_read(res_refs, tiles, pl.program_id(0))
    for a_ref, w_ref in zip(refs[:n_in], refs[n_in:2 * n_in]):
        acc = acc + _dot(a_ref[...], w_ref[...])
    o_ref[...] = acc


def _mm_res(a_list, w, res_parts, tm, tn):
    n = w.shape[1]
    tiles = _part_tiles(res_parts, tm)
    m = sum(tiles) * tm
    n_in = len(a_list)
    kb = a_list[0].shape[1]
    assert all(a.shape == (m, kb) for a in a_list) and w.shape[0] == n_in * kb
    w_list = [w] * n_in
    in_specs = [pl.BlockSpec((tm, kb), lambda i, j: (i, 0)) for _ in a_list]
    in_specs += [pl.BlockSpec((kb, tn), functools.partial(lambda i, j, r: (r, j), r=r))
                 for r in range(n_in)]
    in_specs += _part_specs(res_parts, tm, tn, lambda i, j: j)
    return pl.pallas_call(
        functools.partial(_mm_res_kernel, n_in=n_in, tiles=tiles),
        grid=(m // tm, n // tn),
        in_specs=in_specs,
        out_specs=pl.BlockSpec((tm, tn), lambda i, j: (i, j)),
        out_shape=jax.ShapeDtypeStruct((m, n), F32),
        compiler_params=_params("parallel", "arbitrary"),
        name="mm_residual",
    )(*a_list, *w_list, *res_parts)


HALO = 16


def _ffn_up_kernel(x_ref, xp_ref, xn_ref, g_ref, wg_ref, wu_ref, cw_ref, cb_ref, o_ref, n_ref,
                   *, tm, tiles_per_seq):
    i = pl.program_id(0)

    @pl.when(pl.program_id(1) == 0)
    def _():
        g = g_ref[...]
        first = (i % tiles_per_seq) == 0
        last = (i % tiles_per_seq) == tiles_per_seq - 1
        n_ref[0:HALO, :] = jnp.where(first, 0.0, _rms(xp_ref[...], g)).astype(BF16)
        n_ref[HALO:HALO + tm, :] = _rms(x_ref[...], g).astype(BF16)
        n_ref[HALO + tm:, :] = jnp.where(last, 0.0, _rms(xn_ref[...], g)).astype(BF16)

    ge = _dot(n_ref[...], wg_ref[...])
    rows = tm + 2 * HALO
    prev = pltpu.roll(ge, 1, 0)[HALO:HALO + tm, :]
    nxt = pltpu.roll(ge, rows - 1, 0)[HALO:HALO + tm, :]
    cw = cw_ref[...]
    gate = prev * cw[0:1, :] + ge[HALO:HALO + tm, :] * cw[1:2, :] + nxt * cw[2:3, :] + cb_ref[...]
    up = _dot(n_ref[HALO:HALO + tm, :], wu_ref[...])
    o_ref[...] = (gate * _sigmoid(gate) * up).astype(BF16)


def _ffn_up(h, g, w_up, conv_w, conv_b, seq):
    m, k = h.shape
    tm = _pick(seq, 1024, 512, 256, 128)
    tf = 512
    nf = D_FF // tf
    tps = seq // tm
    r = tm // HALO
    nblk = m // HALO
    return pl.pallas_call(
        functools.partial(_ffn_up_kernel, tm=tm, tiles_per_seq=tps),
        grid=(m // tm, nf),
        in_specs=[
            pl.BlockSpec((tm, k), lambda i, j: (i, 0)),
            pl.BlockSpec((HALO, k), lambda i, j: (jnp.maximum(i * r - 1, 0), 0)),
            pl.BlockSpec((HALO, k), lambda i, j: (jnp.minimum((i + 1) * r, nblk - 1), 0)),
            pl.BlockSpec((1, k), lambda i, j: (0, 0)),
            pl.BlockSpec((k, tf), lambda i, j: (0, j)),
            pl.BlockSpec((k, tf), lambda i, j: (0, j + nf)),
            pl.BlockSpec((3, tf), lambda i, j: (0, j)),
            pl.BlockSpec((1, tf), lambda i, j: (0, j)),
        ],
        out_specs=pl.BlockSpec((tm, tf), lambda i, j: (i, j)),
        out_shape=jax.ShapeDtypeStruct((m, D_FF), BF16),
        scratch_shapes=[pltpu.VMEM((tm + 2 * HALO, k), BF16)],
        compiler_params=_params("parallel", "arbitrary"),
        name="ffn_up",
    )(h, h, h, g.reshape(1, k), w_up, w_up, conv_w, conv_b.reshape(1, D_FF))


def _ple_kernel(h_ref, *refs, split, tn, p_tiles):
    p_refs = refs[:len(p_tiles)]
    g_ref, wg_ref, wp_ref, fn_ref = refs[len(p_tiles):len(p_tiles) + 4]
    out_refs = refs[len(p_tiles) + 4:]
    x = h_ref[...]
    xn = _rms(x, g_ref[...]).astype(BF16)
    pb = _part_read(p_refs, p_tiles, pl.program_id(0)).astype(BF16)
    y_ref = out_refs[-1]
    for c in range(D_MODEL // tn):
        sl = slice(c * tn, (c + 1) * tn)
        gate = _dot(xn, wg_ref[:, sl])
        y_ref[:, sl] = x[:, sl] + _dot(pb, wp_ref[:, sl]) * _sigmoid(gate)
    if split is not None:
        first_ref, second_ref, _ = out_refs
        y = _rms(y_ref[...], fn_ref[...])
        i = pl.program_id(0)

        @pl.when(i < split)
        def _():
            first_ref[...] = y

        @pl.when(i >= split)
        def _():
            second_ref[...] = y


def _ple(h, p_parts, layer, g, wg, wp, fn, m_first=None):
    m, k = h.shape
    tm = _pick(math.gcd(*[p.shape[0] // DEPTH for p in p_parts]), 512, 256, 128)
    p_tiles = tuple(p.shape[0] // DEPTH // tm for p in p_parts)
    assert sum(p_tiles) * tm == m and (m_first is None or m_first % tm == 0)
    p_specs, start = [], 0
    for nt in p_tiles:
        p_specs.append(pl.BlockSpec((tm, PLE_DIM), functools.partial(
            lambda i, s, nt: (layer * nt + jnp.clip(i - s, 0, nt - 1), 0), s=start, nt=nt)))
        start += nt
    tile = pl.BlockSpec((tm, k), lambda i: (i, 0))
    if m_first is None:
        split, scratch = None, []
        out_specs, out_shape = tile, jax.ShapeDtypeStruct((m, k), F32)
    else:
        split, scratch = m_first // tm, [pltpu.VMEM((tm, k), F32)]
        out_specs = [pl.BlockSpec((tm, k), lambda i: (jnp.minimum(i, split - 1), 0)),
                     pl.BlockSpec((tm, k), lambda i: (jnp.maximum(i - split, 0), 0))]
        out_shape = [jax.ShapeDtypeStruct((m_first, k), F32), jax.ShapeDtypeStruct((m - m_first, k), F32)]
    return pl.pallas_call(
        functools.partial(_ple_kernel, split=split, tn=512, p_tiles=p_tiles),
        grid=(m // tm,),
        in_specs=[tile] + p_specs + [
            pl.BlockSpec((1, k), lambda i: (0, 0)),
            pl.BlockSpec(wg.shape, lambda i: (0, 0)),
            pl.BlockSpec(wp.shape, lambda i: (0, 0)),
            pl.BlockSpec((1, k), lambda i: (0, 0)),
        ],
        out_specs=out_specs,
        out_shape=out_shape,
        scratch_shapes=scratch,
        compiler_params=_params("arbitrary"),
        name="ple_gate",
    )(h, *p_parts, g.reshape(1, k), wg, wp, fn.reshape(1, k))


def _rot_cols(w):
    half = ROPE_DIM // 2
    return jnp.concatenate([-w[..., half:], w[..., :half]], axis=-1)


def _pad_rope(w):
    return jnp.concatenate([w, jnp.zeros_like(w)], axis=-1)


def _prepare(attn_norm, w_in, hg_lower, hg_norm, w_q_b, w_kv_b, w_out, w_up, w_down,
             w_ple_gate, w_ple_proj):
    w_kr = w_in[:, :, OFF_KR:]
    wq = w_q_b.reshape(DEPTH, LORA, HEADS, HEAD_DIM + ROPE_DIM)
    wq_r = wq[..., HEAD_DIM:]
    wq = jnp.concatenate([wq[..., :HEAD_DIM], _pad_rope(wq_r), _pad_rope(_rot_cols(wq_r))], axis=-1)
    wkv = w_kv_b.reshape(DEPTH, LORA, HEADS, 2 * HEAD_DIM)

    lb = jnp.cumsum(jax.nn.softmax(hg_lower.astype(F32), axis=1), axis=1)
    lb = jnp.maximum(lb - lb[:, :1], 0.0)
    loglb = jnp.maximum(jnp.log(lb) * LOG2E, NEG).reshape(2, DEPTH, HEADS, HEAD_DIM)
    log1m = (jnp.log1p(-lb) * LOG2E).reshape(2, DEPTH, HEADS, HEAD_DIM)
    gain = hg_norm.astype(F32).reshape(DEPTH, HEADS, HEAD_DIM)
    zeros = jnp.zeros_like(gain)
    crow = jnp.stack([loglb[0], log1m[0], loglb[1], log1m[1], gain, zeros, zeros, zeros], axis=2)

    return dict(
        w_in=w_in[:, :, :OFF_KR].astype(BF16),
        w_kr=jnp.concatenate([_pad_rope(w_kr), _pad_rope(_rot_cols(w_kr))], axis=-1).astype(BF16),
        w_q=wq.reshape(DEPTH, LORA, HEADS * 3 * LANES).astype(BF16),
        w_k=wkv[..., :HEAD_DIM].reshape(DEPTH, LORA, HG_WIDTH).astype(BF16),
        w_vt=jnp.swapaxes(wkv[..., HEAD_DIM:].reshape(DEPTH, LORA, HG_WIDTH), 1, 2).astype(BF16),
        w_out=w_out.astype(BF16),
        w_up=w_up.astype(BF16),
        w_down=w_down.astype(BF16),
        w_ple_gate=w_ple_gate.astype(BF16),
        w_ple_proj=w_ple_proj.astype(BF16),
        crow=crow,
    )


def _rope_tables(seq):
    inv = ROPE_THETA ** (-jnp.arange(0, ROPE_DIM, 2, dtype=F32) / ROPE_DIM)
    ang = jnp.arange(seq, dtype=F32)[:, None] * inv[None, :]
    pad = jnp.zeros((seq, LANES - ROPE_DIM), F32)
    cosz = jnp.concatenate([jnp.cos(ang), jnp.cos(ang), pad], axis=-1)
    sinz = jnp.concatenate([jnp.sin(ang), jnp.sin(ang), pad], axis=-1)
    return cosz, sinz


def _trunk(xs, ps, attn_norm, q_a_norm, kv_a_norm, ffn_norm, conv_w, conv_b, ple_norm, final_norm, prm):
    seq, d = xs[0].shape[1:]
    nbs = [x.shape[0] for x in xs]
    nb = sum(nbs)
    h_parts = [x.reshape(n * seq, d) for x, n in zip(xs, nbs)]
    p_parts = [p.reshape(DEPTH * n * seq, PLE_DIM) for p, n in zip(ps, nbs)]
    cosz, sinz = _rope_tables(seq)
    tm = _pick(seq, 1024, 512, 256, 128)
    tk = _pick(seq, 1024, 512, 256, 128)
    for l in range(DEPTH):
        z, zkr = _norm_mm(h_parts, attn_norm[l], prm["w_in"][l], prm["w_kr"][l], tm,
                          1024 // len(h_parts))
        qp = _qproj(z, q_a_norm[l], prm["w_q"][l], cosz, sinz, seq)
        kp, vt = _kvproj(z, zkr, kv_a_norm[l], prm["w_k"][l], prm["w_vt"][l], cosz, sinz, nb, seq, tk)
        o_hg, o_mla = _mixer(z, prm["crow"][l], qp, kp, vt, nb, seq, tk)
        h = _mm_res([o_hg, o_mla], prm["w_out"][l], h_parts, tm, 512)
        act = _ffn_up(h, ffn_norm[l], prm["w_up"][l], conv_w[l], conv_b[l], seq)
        h = _mm_res([act], prm["w_down"][l], [h], tm, 512)
        h = _ple(h, p_parts, l, ple_norm[l], prm["w_ple_gate"][l], prm["w_ple_proj"][l], final_norm,
                 m_first=nbs[0] * seq if l == DEPTH - 1 else None)
        h_parts = [h]
    return h[0].reshape(nbs[0], seq, d), h[1].reshape(nbs[1], seq, d)


def kernel(x_prompt, x_sample, p_prompt, p_sample, attn_norm, w_in, hg_lower, hg_norm, q_a_norm,
           w_q_b, kv_a_norm, w_kv_b, w_out, ffn_norm, w_up, conv_w, conv_b, w_down, ple_norm,
           w_ple_gate, w_ple_proj, final_norm):
    assert x_prompt.shape[1:] == x_sample.shape[1:], "both groups must share the sequence length"
    prm = _prepare(attn_norm, w_in, hg_lower, hg_norm, w_q_b, w_kv_b, w_out, w_up, w_down,
                   w_ple_gate, w_ple_proj)
    return _trunk([x_prompt, x_sample], [p_prompt, p_sample], attn_norm, q_a_norm, kv_a_norm,
                  ffn_norm, conv_w, conv_b, ple_norm, final_norm, prm)
```

```python
import functools
import math

import jax
import jax.numpy as jnp
import numpy as np
from jax import lax
from jax.experimental import pallas as pl
from jax.experimental.pallas import tpu as pltpu

F32 = jnp.float32
BF16 = jnp.bfloat16

D_MODEL = 2048
DEPTH = 4
HEADS = 8
HEAD_DIM = 128
ROPE_DIM = 64
LORA = 512
HG_WIDTH = HEADS * HEAD_DIM
OFF_Q, OFF_FF, OFF_FB, OFF_I, OFF_G, OFF_QA, OFF_KVA, OFF_KR = (
    0, 1024, 2048, 3072, 4096, 5120, 5632, 6144)
D_FF = 5632
PLE_DIM = 256
EPS = 1e-6
ROPE_THETA = 10000.0
NEG = -1e30

LANES = 128
SUBLANES = 8
VMEM_LIMIT = 56 * 2**20

HG_CHUNK = 64
HG_ROWS = SUBLANES
HG_STEP = 512
HG_UNROLL = 8
Q_PAD = 2 * LANES
LOG2E = math.log2(math.e)
QK_SCALE = (HEAD_DIM + ROPE_DIM) ** -0.5 * LOG2E
ATT_GROUP = 2 * LANES
ATT_BLOCKS = 2
VT_ROWS = HEAD_DIM + 16
MIX_ROUNDS = 2


def _params(*sem):
    return pltpu.CompilerParams(dimension_semantics=sem, vmem_limit_bytes=VMEM_LIMIT)


def _pick(n, *cands):
    for c in cands:
        if n % c == 0:
            return c
    raise ValueError(f"no tile for {n} in {cands}")


def _rms(x, g):
    ms = jnp.mean(x * x, axis=-1, keepdims=True)
    return x * lax.rsqrt(ms + EPS) * g


def _sigmoid(x):
    return 1.0 / (1.0 + jnp.exp(-x))


def _dot(a, b):
    return jnp.dot(a, b, preferred_element_type=F32)


def _dot_nt(a, b):
    return lax.dot_general(a, b, (((1,), (1,)), ((), ())), preferred_element_type=F32)


def _dot_tn(a, b):
    return lax.dot_general(a, b, (((0,), (0,)), ((), ())), preferred_element_type=F32)


def _part_tiles(parts, tm):
    assert all(p.shape[0] % tm == 0 for p in parts)
    return tuple(p.shape[0] // tm for p in parts)


def _part_specs(parts, tm, width, col):
    specs, start = [], 0
    for nt in _part_tiles(parts, tm):
        def imap(i, *rest, start=start, nt=nt):
            inside = (i >= start) & (i < start + nt)
            return jnp.clip(i - start, 0, nt - 1), jnp.where(inside, col(i, *rest), 0)
        specs.append(pl.BlockSpec((tm, width), imap))
        start += nt
    return specs


def _part_read(refs, tiles, i):
    x, start = refs[0][...], tiles[0]
    for ref, nt in zip(refs[1:], tiles[1:]):
        x = jnp.where(i >= start, ref[...], x)
        start += nt
    return x


def _norm_mm_kernel(*refs, tiles):
    x_refs = refs[:len(tiles)]
    g_ref, w_ref, ws_ref, o_ref, os_ref, xn_ref = refs[len(tiles):]

    @pl.when(pl.program_id(1) == 0)
    def _():
        x = _part_read(x_refs, tiles, pl.program_id(0))
        xn_ref[...] = _rms(x, g_ref[...]).astype(BF16)
        os_ref[...] = _dot(xn_ref[...], ws_ref[...])

    o_ref[...] = _dot(xn_ref[...], w_ref[...])


def _norm_mm(x_parts, g, w, w_side, tm, tn):
    k, n = w.shape
    ns = w_side.shape[1]
    tiles = _part_tiles(x_parts, tm)
    m = sum(tiles) * tm
    return pl.pallas_call(
        functools.partial(_norm_mm_kernel, tiles=tiles),
        grid=(m // tm, n // tn),
        in_specs=_part_specs(x_parts, tm, k, lambda i, j: 0) + [
            pl.BlockSpec((1, k), lambda i, j: (0, 0)),
            pl.BlockSpec((k, tn), lambda i, j: (0, j)),
            pl.BlockSpec((k, ns), lambda i, j: (0, 0)),
        ],
        out_specs=[pl.BlockSpec((tm, tn), lambda i, j: (i, j)),
                   pl.BlockSpec((tm, ns), lambda i, j: (i, 0))],
        out_shape=[jax.ShapeDtypeStruct((m, n), F32), jax.ShapeDtypeStruct((m, ns), F32)],
        scratch_shapes=[pltpu.VMEM((tm, k), BF16)],
        compiler_params=_params("parallel", "arbitrary"),
        name="norm_mm",
    )(*x_parts, g.reshape(1, k), w, w_side)


def _hgrn_gates(z, loglb, log1m):
    z2 = z * LOG2E
    l1p = jnp.log2(1.0 + jnp.exp2(-jnp.abs(z2)))
    c = log1m + (jnp.minimum(z2, 0.0) - l1p)
    logf = jnp.maximum(loglb, c) + jnp.log2(1.0 + jnp.exp2(-jnp.abs(loglb - c)))
    return logf, c - z2


def _hgrn_chunk(q, zf, v, cmat, lvl, emat, loglb, log1m, st_ref, cb_ref, reverse):
    c = q.shape[0]
    nv = c // HG_ROWS
    logf, logk = _hgrn_gates(zf, loglb, log1m)

    hi = logf.astype(BF16)
    lo = (logf - hi.astype(F32)).astype(BF16)
    yield
    bb = _dot(cmat, jnp.concatenate([hi, lo], axis=1))
    b = bb[:, :LANES] + bb[:, LANES:]

    def rows(x, p):
        i = nv - 1 - p if reverse else p
        return x[i * HG_ROWS:(i + 1) * HG_ROWS, :]

    def assemble(groups):
        return jnp.concatenate(groups[::-1] if reverse else groups, axis=0)

    edge = 0 if reverse else HG_ROWS - 1
    b_p = [rows(b, p) for p in range(nv)]
    q_p = [rows(q, p) for p in range(nv)]
    lk_p = [rows(logk, p) for p in range(nv)]
    bnd = [jnp.broadcast_to(x[edge:edge + 1, :], (HG_ROWS, LANES)) for x in b_p]
    zero = jnp.zeros((HG_ROWS, LANES), F32)

    a = jnp.zeros((c, c), F32)
    level, mv = 1, nv // 2
    while mv >= 1:
        qm = [zero] * nv
        km = [zero] * nv
        for g in range(nv // (2 * mv)):
            ref = bnd[2 * mv * g + mv - 1]
            for p in range(2 * mv * g, 2 * mv * g + mv):
                km[p] = jnp.exp2(lk_p[p] + (ref - b_p[p]))
            for p in range(2 * mv * g + mv, 2 * mv * (g + 1)):
                qm[p] = q_p[p] * jnp.exp2(b_p[p] - ref)
        yield
        al = _dot_nt(assemble(qm).astype(BF16), assemble(km).astype(BF16))
        a = jnp.where(lvl == float(level), al, a)
        level += 1
        mv //= 2

    cb_ref[...] = b - logk
    slabs = []
    for s in range(HG_ROWS):
        col = []
        for p in range(nv):
            r = (nv - 1 - p if reverse else p) * HG_ROWS + s
            cs = jnp.broadcast_to(cb_ref[r:r + 1, :], (HG_ROWS, LANES))
            col.append(q_p[p] * jnp.exp2(jnp.minimum(b_p[p] - cs, 0.0)))
        slabs.append(assemble(col).astype(BF16))
        yield
    adiag = yield jnp.concatenate(slabs, axis=1)
    a = jnp.where(lvl == float(level), adiag, a)

    tot = bnd[nv - 1]
    qbar = assemble([q_p[p] * jnp.exp2(b_p[p]) for p in range(nv)]).astype(BF16)
    kbar = assemble([jnp.exp2(lk_p[p] + (tot - b_p[p])) for p in range(nv)]).astype(BF16)
    vb = v.astype(BF16)
    yield
    st = st_ref[...]
    o = _dot(jnp.concatenate([qbar, a.astype(BF16)], axis=1),
             jnp.concatenate([st.T.astype(BF16), vb], axis=0))
    st_ref[...] = jnp.exp2(tot[0:1, :]) * st + _dot_tn(vb, kbar)
    return o


def _lockstep(gens):
    results = [None] * len(gens)
    live = list(range(len(gens)))
    while live:
        for idx in list(live):
            try:
                next(gens[idx])
            except StopIteration as done:
                results[idx] = done.value
                live.remove(idx)
    return results


def _hgrn_kernel(qf_ref, zf_ref, vf_ref, gf_ref, qb_ref, zb_ref, vb_ref, gb_ref,
                 crow_ref, cmat_ref, lvl_ref, emat_ref, o_ref,
                 part_ref, stf_ref, stb_ref, cb_ref, *, step, chunk, nsteps):
    c = pl.program_id(2)

    @pl.when(c == 0)
    def _():
        stf_ref[...] = jnp.zeros_like(stf_ref)
        stb_ref[...] = jnp.zeros_like(stb_ref)
        part_ref[...] = jnp.zeros_like(part_ref)

    crow = crow_ref[0]
    gain = crow[4:5, :]
    emat = emat_ref[...]
    nch = step // chunk

    def emit(o, row, g):
        t = o + part_ref[pl.ds(row, chunk), :]
        part_ref[pl.ds(row, chunk), :] = t
        o_ref[pl.ds(row, chunk), :] = (_rms(t, gain) * (g * _sigmoid(g))).astype(BF16)

    def fwd(j, u):
        rf = pl.multiple_of(j * chunk, chunk)
        sl = pl.ds(rf, chunk)
        gen = _hgrn_chunk(qf_ref[sl, :], zf_ref[sl, :], vf_ref[sl, :], cmat_ref[0], lvl_ref[0], emat,
                          crow[0:1, :], crow[1:2, :], stf_ref, cb_ref.at[0, u], False)
        return gen, pl.multiple_of(c * step + rf, chunk), gf_ref, sl

    def bwd(j, u):
        rb = pl.multiple_of((nch - 1 - j) * chunk, chunk)
        sl = pl.ds(rb, chunk)
        gen = _hgrn_chunk(qb_ref[sl, :], zb_ref[sl, :], vb_ref[sl, :], cmat_ref[1], lvl_ref[1], emat,
                          crow[2:3, :], crow[3:4, :], stb_ref, cb_ref.at[1, u], True)
        return gen, pl.multiple_of((nsteps - 1 - c) * step + rb, chunk), gb_ref, sl

    unroll = math.gcd(HG_UNROLL, nch)

    def body(i, carry):
        chains = [f(i * unroll + u, u) for u in range(unroll) for f in (fwd, bwd)]
        outs = _lockstep([ch[0] for ch in chains])
        for o, (_, row, g_ref, sl) in zip(outs, chains):
            emit(o, row, g_ref[sl, :])
        return carry

    lax.fori_loop(0, nch // unroll, body, 0)


def _hgrn_consts(chunk):
    t = np.arange(chunk)[:, None]
    s = np.arange(chunk)[None, :]
    cm = np.stack([(s <= t), (s >= t)]).astype(np.float32)
    nlev = int(round(math.log2(chunk // HG_ROWS)))
    lv = np.zeros((chunk, chunk), np.float32)
    for level in range(1, nlev + 1):
        m = chunk >> level
        hit = (t // (2 * m) == s // (2 * m)) & (t // m != s // m) & (s < t)
        lv[hit] = level
    lv[((t // HG_ROWS) == (s // HG_ROWS)) & (s <= t)] = nlev + 1
    lvl = np.stack([lv, lv.T]).astype(np.float32)
    em = (np.arange(HG_ROWS * LANES)[:, None] // LANES == (np.arange(chunk)[None, :] % HG_ROWS))
    return jnp.asarray(cm, BF16), jnp.asarray(lvl, F32), jnp.asarray(em.astype(np.float32), BF16)


def _hgrn(z, crow, nb, seq):
    m = z.shape[0]
    step = _pick(seq // 2, HG_STEP, 256, 128, 64)
    chunk = HG_CHUNK
    n = seq // step
    half = n // 2
    cmat, lvl, emat = _hgrn_consts(chunk)

    def zspec(col, fwd, gate=False):
        if fwd:
            blk = (lambda c: jnp.maximum(c, half)) if gate else (lambda c: c)
        else:
            blk = (lambda c: jnp.minimum(n - 1 - c, half - 1)) if gate else (lambda c: n - 1 - c)
        return pl.BlockSpec((step, LANES), lambda b, h, c: (b * n + blk(c), col // LANES + h))

    const3 = lambda shape: pl.BlockSpec(shape, lambda b, h, c: (0, 0, 0))
    return pl.pallas_call(
        functools.partial(_hgrn_kernel, step=step, chunk=chunk, nsteps=n),
        grid=(nb, HEADS, n),
        in_specs=[
            zspec(OFF_Q, True), zspec(OFF_FF, True), zspec(OFF_I, True), zspec(OFF_G, True, True),
            zspec(OFF_Q, False), zspec(OFF_FB, False), zspec(OFF_I, False), zspec(OFF_G, False, True),
            pl.BlockSpec((1, SUBLANES, LANES), lambda b, h, c: (h, 0, 0)),
            const3(cmat.shape), const3(lvl.shape),
            pl.BlockSpec(emat.shape, lambda b, h, c: (0, 0)),
        ],
        out_specs=pl.BlockSpec((seq, LANES), lambda b, h, c: (b, h)),
        out_shape=jax.ShapeDtypeStruct((m, HG_WIDTH), BF16),
        scratch_shapes=[
            pltpu.VMEM((seq, LANES), F32),
            pltpu.VMEM((HEAD_DIM, HEAD_DIM), F32),
            pltpu.VMEM((HEAD_DIM, HEAD_DIM), F32),
            pltpu.VMEM((2, HG_UNROLL, chunk, LANES), F32),
        ],
        compiler_params=_params("parallel", "parallel", "arbitrary"),
        name="hgrn2",
    )(z, z, z, z, z, z, z, z, crow, cmat, lvl, emat)


def _qproj_kernel(z_ref, g_ref, w_ref, cos_ref, sin_ref, o_ref):
    xn = _rms(z_ref[...], g_ref[...]).astype(BF16)
    cz = cos_ref[...]
    sz = sin_ref[...]
    for h in range(HEADS):
        acc = _dot(xn, w_ref[:, h * 3 * LANES:(h + 1) * 3 * LANES])
        o_ref[:, h * Q_PAD:h * Q_PAD + LANES] = (acc[:, :LANES] * QK_SCALE).astype(BF16)
        rope = acc[:, LANES:2 * LANES] * cz + acc[:, 2 * LANES:] * sz
        o_ref[:, h * Q_PAD + LANES:(h + 1) * Q_PAD] = (rope * QK_SCALE).astype(BF16)


def _qproj(z, g, w, cosz, sinz, seq):
    m = z.shape[0]
    tm = _pick(seq, 512, 256, 128)
    ns = seq // tm
    return pl.pallas_call(
        _qproj_kernel,
        grid=(m // tm,),
        in_specs=[
            pl.BlockSpec((tm, LORA), lambda i: (i, OFF_QA // LORA)),
            pl.BlockSpec((1, LORA), lambda i: (0, 0)),
            pl.BlockSpec(w.shape, lambda i: (0, 0)),
            pl.BlockSpec((tm, LANES), lambda i: (i % ns, 0)),
            pl.BlockSpec((tm, LANES), lambda i: (i % ns, 0)),
        ],
        out_specs=pl.BlockSpec((tm, HEADS * Q_PAD), lambda i: (i, 0)),
        out_shape=jax.ShapeDtypeStruct((m, HEADS * Q_PAD), BF16),
        compiler_params=_params("parallel"),
        name="mla_q_proj",
    )(z, g.reshape(1, LORA), w, cosz, sinz)


def _kvproj_kernel(z_ref, kr_ref, g_ref, wk_ref, wvt_ref, cos_ref, sin_ref, k_ref, vt_ref):
    xn = _rms(z_ref[...], g_ref[...]).astype(BF16)
    kr = kr_ref[...]
    krope = (kr[:, :LANES] * cos_ref[...] + kr[:, LANES:] * sin_ref[...]).astype(BF16)
    kn = _dot(xn, wk_ref[...])
    for h in range(HEADS):
        k_ref[:, h * Q_PAD:h * Q_PAD + LANES] = kn[:, h * LANES:(h + 1) * LANES].astype(BF16)
        k_ref[:, h * Q_PAD + LANES:(h + 1) * Q_PAD] = krope
    vt = _dot_nt(wvt_ref[...], xn).astype(BF16)
    ones = jnp.ones((VT_ROWS - HEAD_DIM, vt.shape[1]), BF16)
    for h in range(HEADS):
        vt_ref[0, 0, h, :HEAD_DIM, :] = vt[h * HEAD_DIM:(h + 1) * HEAD_DIM, :]
        vt_ref[0, 0, h, HEAD_DIM:, :] = ones


def _kvproj(z, zkr, g, wk, wvt, cosz, sinz, nb, seq, tk):
    m = z.shape[0]
    ns = seq // tk
    return pl.pallas_call(
        _kvproj_kernel,
        grid=(m // tk,),
        in_specs=[
            pl.BlockSpec((tk, LORA), lambda i: (i, OFF_KVA // LORA)),
            pl.BlockSpec((tk, 2 * LANES), lambda i: (i, 0)),
            pl.BlockSpec((1, LORA), lambda i: (0, 0)),
            pl.BlockSpec(wk.shape, lambda i: (0, 0)),
            pl.BlockSpec(wvt.shape, lambda i: (0, 0)),
            pl.BlockSpec((tk, LANES), lambda i: (i % ns, 0)),
            pl.BlockSpec((tk, LANES), lambda i: (i % ns, 0)),
        ],
        out_specs=[
            pl.BlockSpec((tk, HEADS * Q_PAD), lambda i: (i, 0)),
            pl.BlockSpec((1, 1, HEADS, VT_ROWS, tk), lambda i: (i // ns, i % ns, 0, 0, 0)),
        ],
        out_shape=[
            jax.ShapeDtypeStruct((m, HEADS * Q_PAD), BF16),
            jax.ShapeDtypeStruct((nb, ns, HEADS, VT_ROWS, tk), BF16),
        ],
        compiler_params=_params("parallel"),
        name="mla_kv_proj",
    )(z, zkr, g.reshape(1, LORA), wk, wvt, cosz, sinz)


def _col_reduce(x, op, slab=64):
    r, c = x.shape
    if r > slab and r % slab == 0:
        x = op(x.reshape(r // slab, slab, c), axis=0)
    return op(x, axis=0, keepdims=True)


def _attn_kernel(q_ref, k_ref, vt_ref, o_ref, acc_ref, st_ref, mx_ref, *, tk, nk, nblk):
    tq = q_ref.shape[0]
    nslot, ng, _, gw = st_ref.shape
    assert nblk % nslot == 0 and nk % nblk == 0
    acc_ref[...] = jnp.zeros_like(acc_ref)

    def scores(j, slot, g):
        kb = k_ref[pl.ds(pl.multiple_of(j * tk, tk), tk), :]
        st = _dot_nt(kb, q_ref[g * gw:(g + 1) * gw, :])
        st_ref[slot, g] = st
        mx_ref[slot, g] = _col_reduce(st, jnp.max)

    def consume(j, slot, g, m):
        cols = slice(g * gw, (g + 1) * gw)
        m_new = jnp.maximum(m, mx_ref[slot, g])
        alpha = jnp.exp2(m - m_new)
        p = jnp.exp2((st_ref[slot, g] - m_new).astype(BF16))
        acc_ref[:, cols] = alpha * acc_ref[:, cols] + _dot(vt_ref[0, j, 0], p)
        return m_new

    def trip(j, ms, last):
        ms = list(ms)
        for u in range(nblk):
            for g in range(ng):
                if not (last and u == nblk - 1):
                    scores(j + u + 1, (u + 1) % nslot, g)
                ms[g] = consume(j + u, u % nslot, g, ms[g])
        return tuple(ms)

    for g in range(ng):
        scores(0, 0, g)
    init = tuple(jnp.full((1, gw), NEG, F32) for _ in range(ng))
    ms = lax.fori_loop(0, nk // nblk - 1, lambda i, ms: trip(nblk * i, ms, False), init)
    trip(nk - nblk, ms, True)
    for g in range(ng):
        cols = slice(g * gw, (g + 1) * gw)
        o_ref[cols, :] = (acc_ref[:HEAD_DIM, cols] / acc_ref[HEAD_DIM:HEAD_DIM + 1, cols]).T.astype(BF16)


def _attention(qp, kp, vt, nb, seq, tk):
    m = qp.shape[0]
    tq = _pick(seq, 1024, 512, 256, 128)
    nq = seq // tq
    nk = seq // tk
    nblk = ATT_BLOCKS if nk % ATT_BLOCKS == 0 else 2
    assert nk % nblk == 0, "the key loop handles an even number of blocks per trip"
    gw = min(ATT_GROUP, tq)
    return pl.pallas_call(
        functools.partial(_attn_kernel, tk=tk, nk=nk, nblk=nblk),
        grid=(nb, HEADS, nq),
        in_specs=[
            pl.BlockSpec((tq, Q_PAD), lambda b, h, i: (b * nq + i, h)),
            pl.BlockSpec((seq, Q_PAD), lambda b, h, i: (b, h)),
            pl.BlockSpec((1, nk, 1, VT_ROWS, tk), lambda b, h, i: (b, 0, h, 0, 0)),
        ],
        out_specs=pl.BlockSpec((tq, HEAD_DIM), lambda b, h, i: (b * nq + i, h)),
        out_shape=jax.ShapeDtypeStruct((m, HG_WIDTH), BF16),
        scratch_shapes=[pltpu.VMEM((VT_ROWS, tq), F32), pltpu.VMEM((2, tq // gw, tk, gw), F32),
                        pltpu.VMEM((2, tq // gw, 1, gw), F32)],
        compiler_params=_params("parallel", "parallel", "arbitrary"),
        name="mla_attention",
    )(qp, kp, vt)


def _interleave(units, gens, rounds, emat):
    results = [None] * len(gens)
    inbox = [None] * len(gens)
    live = list(range(len(gens)))

    def advance():
        asked = []
        for idx in list(live):
            try:
                out = gens[idx].send(inbox[idx])
            except StopIteration as done:
                results[idx] = done.value
                live.remove(idx)
                continue
            inbox[idx] = None
            if out is not None:
                asked.append((idx, out))
        if asked:
            prod = _dot(jnp.concatenate([x for _, x in asked], axis=0), emat)
            row = 0
            for idx, x in asked:
                inbox[idx] = prod[row:row + x.shape[0], :]
                row += x.shape[0]

    for unit in units:
        unit()
        for _ in range(rounds):
            advance()
    while live:
        advance()
    return results


def _mixer_kernel(qf_ref, zf_ref, vf_ref, gf_ref, qb_ref, zb_ref, vb_ref, gb_ref,
                  crow_ref, cmat_ref, lvl_ref, emat_ref, q_ref, k_ref, vt_ref,
                  ohg_ref, oat_ref,
                  part_ref, stf_ref, stb_ref, cb_ref, acc_ref, st_ref, mx_ref,
                  *, chunk, tk, nk, nblk):
    c = pl.program_id(2)
    nsteps = pl.num_programs(2)
    tq = q_ref.shape[0]
    nslot, ng, _, gw = st_ref.shape
    ntrips = nk // nblk
    nch = tq // chunk
    assert nblk % nslot == 0 and nk % nblk == 0 and nch % ntrips == 0
    hpt = nch // ntrips

    @pl.when(c == 0)
    def _():
        stf_ref[...] = jnp.zeros_like(stf_ref)
        stb_ref[...] = jnp.zeros_like(stb_ref)
        part_ref[...] = jnp.zeros_like(part_ref)

    acc_ref[...] = jnp.zeros_like(acc_ref)
    crow = crow_ref[0]
    gain = crow[4:5, :]
    emat = emat_ref[...]

    def emit(o, row, g):
        t = o + part_ref[pl.ds(row, chunk), :]
        part_ref[pl.ds(row, chunk), :] = t
        ohg_ref[pl.ds(row, chunk), :] = (_rms(t, gain) * (g * _sigmoid(g))).astype(BF16)

    def fwd(j, u):
        rf = pl.multiple_of(j * chunk, chunk)
        sl = pl.ds(rf, chunk)
        gen = _hgrn_chunk(qf_ref[sl, :], zf_ref[sl, :], vf_ref[sl, :], cmat_ref[0], lvl_ref[0], emat,
                          crow[0:1, :], crow[1:2, :], stf_ref, cb_ref.at[0, u], False)
        return gen, pl.multiple_of(c * tq + rf, chunk), gf_ref, sl

    def bwd(j, u):
        rb = pl.multiple_of((nch - 1 - j) * chunk, chunk)
        sl = pl.ds(rb, chunk)
        gen = _hgrn_chunk(qb_ref[sl, :], zb_ref[sl, :], vb_ref[sl, :], cmat_ref[1], lvl_ref[1], emat,
                          crow[2:3, :], crow[3:4, :], stb_ref, cb_ref.at[1, u], True)
        return gen, pl.multiple_of((nsteps - 1 - c) * tq + rb, chunk), gb_ref, sl

    def scores(j, slot, g):
        kb = k_ref[pl.ds(pl.multiple_of(j * tk, tk), tk), :]
        st = _dot_nt(kb, q_ref[g * gw:(g + 1) * gw, :])
        st_ref[slot, g] = st
        mx_ref[slot, g] = _col_reduce(st, jnp.max)

    def consume(j, slot, g, m):
        cols = slice(g * gw, (g + 1) * gw)
        m_new = jnp.maximum(m, mx_ref[slot, g])
        alpha = jnp.exp2(m - m_new)
        p = jnp.exp2((st_ref[slot, g] - m_new).astype(BF16))
        acc_ref[:, cols] = alpha * acc_ref[:, cols] + _dot(vt_ref[0, j, 0], p)
        return m_new

    def trip(t, ms, last):
        ms = list(ms)
        j = nblk * t
        units = []
        for u in range(nblk):
            for g in range(ng):
                def unit(u=u, g=g):
                    if not (last and u == nblk - 1):
                        scores(j + u + 1, (u + 1) % nslot, g)
                    ms[g] = consume(j + u, u % nslot, g, ms[g])
                units.append(unit)
        chains = [f(t * hpt + v, v) for v in range(hpt) for f in (fwd, bwd)]
        outs = _interleave(units, [ch[0] for ch in chains], MIX_ROUNDS, emat)
        for o, (_, row, g_ref, sl) in zip(outs, chains):
            emit(o, row, g_ref[sl, :])
        return tuple(ms)

    for g in range(ng):
        scores(0, 0, g)
    init = tuple(jnp.full((1, gw), NEG, F32) for _ in range(ng))
    ms = lax.fori_loop(0, ntrips - 1, lambda t, ms: trip(t, ms, False), init)
    trip(ntrips - 1, ms, True)
    for g in range(ng):
        cols = slice(g * gw, (g + 1) * gw)
        oat_ref[cols, :] = (acc_ref[:HEAD_DIM, cols] / acc_ref[HEAD_DIM:HEAD_DIM + 1, cols]).T.astype(BF16)


def _mixer(z, crow, qp, kp, vt, nb, seq, tk):
    m = z.shape[0]
    tq = _pick(seq, 1024, 512, 256, 128)
    n = seq // tq
    assert n % 2 == 0, "the two scan directions meet in the middle"
    half = n // 2
    nk = seq // tk
    nblk = ATT_BLOCKS if nk % ATT_BLOCKS == 0 else 2
    gw = min(ATT_GROUP, tq)
    chunk = HG_CHUNK
    hpt = tq // chunk // (nk // nblk)
    cmat, lvl, emat = _hgrn_consts(chunk)

    def zspec(col, fwd, gate=False):
        if fwd:
            blk = (lambda c: jnp.maximum(c, half)) if gate else (lambda c: c)
        else:
            blk = (lambda c: jnp.minimum(n - 1 - c, half - 1)) if gate else (lambda c: n - 1 - c)
        return pl.BlockSpec((tq, LANES), lambda b, h, c: (b * n + blk(c), col // LANES + h))

    const3 = lambda shape: pl.BlockSpec(shape, lambda b, h, c: (0, 0, 0))
    return pl.pallas_call(
        functools.partial(_mixer_kernel, chunk=chunk, tk=tk, nk=nk, nblk=nblk),
        grid=(nb, HEADS, n),
        in_specs=[
            zspec(OFF_Q, True), zspec(OFF_FF, True), zspec(OFF_I, True), zspec(OFF_G, True, True),
            zspec(OFF_Q, False), zspec(OFF_FB, False), zspec(OFF_I, False), zspec(OFF_G, False, True),
            pl.BlockSpec((1, SUBLANES, LANES), lambda b, h, c: (h, 0, 0)),
            const3(cmat.shape), const3(lvl.shape),
            pl.BlockSpec(emat.shape, lambda b, h, c: (0, 0)),
            pl.BlockSpec((tq, Q_PAD), lambda b, h, c: (b * n + c, h)),
            pl.BlockSpec((seq, Q_PAD), lambda b, h, c: (b, h)),
            pl.BlockSpec((1, nk, 1, VT_ROWS, tk), lambda b, h, c: (b, 0, h, 0, 0)),
        ],
        out_specs=[pl.BlockSpec((seq, LANES), lambda b, h, c: (b, h)),
                   pl.BlockSpec((tq, HEAD_DIM), lambda b, h, c: (b * n + c, h))],
        out_shape=[jax.ShapeDtypeStruct((m, HG_WIDTH), BF16), jax.ShapeDtypeStruct((m, HG_WIDTH), BF16)],
        scratch_shapes=[
            pltpu.VMEM((seq, LANES), F32),
            pltpu.VMEM((HEAD_DIM, HEAD_DIM), F32),
            pltpu.VMEM((HEAD_DIM, HEAD_DIM), F32),
            pltpu.VMEM((2, hpt, chunk, LANES), F32),
            pltpu.VMEM((VT_ROWS, tq), F32),
            pltpu.VMEM((2, tq // gw, tk, gw), F32),
            pltpu.VMEM((2, tq // gw, 1, gw), F32),
        ],
        compiler_params=_params("parallel", "parallel", "arbitrary"),
        name="token_mixer",
    )(z, z, z, z, z, z, z, z, crow, cmat, lvl, emat, qp, kp, vt)


def _mm_res_kernel(*refs, n_in, tiles):
    res_refs, o_ref = refs[2 * n_in:-1], refs[-1]
    acc = _part_read(res_refs, tiles, pl.program_id(0))
    for a_ref, w_ref in zip(refs[:n_in], refs[n_in:2 * n_in]):
        acc = acc + _dot(a_ref[...], w_ref[...])
    o_ref[...] = acc


def _mm_res(a_list, w, res_parts, tm, tn):
    n = w.shape[1]
    tiles = _part_tiles(res_parts, tm)
    m = sum(tiles) * tm
    n_in = len(a_list)
    kb = a_list[0].shape[1]
    assert all(a.shape == (m, kb) for a in a_list) and w.shape[0] == n_in * kb
    w_list = [w] * n_in
    in_specs = [pl.BlockSpec((tm, kb), lambda i, j: (i, 0)) for _ in a_list]
    in_specs += [pl.BlockSpec((kb, tn), functools.partial(lambda i, j, r: (r, j), r=r))
                 for r in range(n_in)]
    in_specs += _part_specs(res_parts, tm, tn, lambda i, j: j)
    return pl.pallas_call(
        functools.partial(_mm_res_kernel, n_in=n_in, tiles=tiles),
        grid=(m // tm, n // tn),
        in_specs=in_specs,
        out_specs=pl.BlockSpec((tm, tn), lambda i, j: (i, j)),
        out_shape=jax.ShapeDtypeStruct((m, n), F32),
        compiler_params=_params("parallel", "arbitrary"),
        name="mm_residual",
    )(*a_list, *w_list, *res_parts)


HALO = 16


def _ffn_up_kernel(x_ref, xp_ref, xn_ref, g_ref, wg_ref, wu_ref, cw_ref, cb_ref, o_ref, n_ref,
                   *, tm, tiles_per_seq):
    i = pl.program_id(0)

    @pl.when(pl.program_id(1) == 0)
    def _():
        g = g_ref[...]
        first = (i % tiles_per_seq) == 0
        last = (i % tiles_per_seq) == tiles_per_seq - 1
        n_ref[0:HALO, :] = jnp.where(first, 0.0, _rms(xp_ref[...], g)).astype(BF16)
        n_ref[HALO:HALO + tm, :] = _rms(x_ref[...], g).astype(BF16)
        n_ref[HALO + tm:, :] = jnp.where(last, 0.0, _rms(xn_ref[...], g)).astype(BF16)

    ge = _dot(n_ref[...], wg_ref[...])
    rows = tm + 2 * HALO
    prev = pltpu.roll(ge, 1, 0)[HALO:HALO + tm, :]
    nxt = pltpu.roll(ge, rows - 1, 0)[HALO:HALO + tm, :]
    cw = cw_ref[...]
    gate = prev * cw[0:1, :] + ge[HALO:HALO + tm, :] * cw[1:2, :] + nxt * cw[2:3, :] + cb_ref[...]
    up = _dot(n_ref[HALO:HALO + tm, :], wu_ref[...])
    o_ref[...] = (gate * _sigmoid(gate) * up).astype(BF16)


def _ffn_up(h, g, w_up, conv_w, conv_b, seq):
    m, k = h.shape
    tm = _pick(seq, 1024, 512, 256, 128)
    tf = 512
    nf = D_FF // tf
    tps = seq // tm
    r = tm // HALO
    nblk = m // HALO
    return pl.pallas_call(
        functools.partial(_ffn_up_kernel, tm=tm, tiles_per_seq=tps),
        grid=(m // tm, nf),
        in_specs=[
            pl.BlockSpec((tm, k), lambda i, j: (i, 0)),
            pl.BlockSpec((HALO, k), lambda i, j: (jnp.maximum(i * r - 1, 0), 0)),
            pl.BlockSpec((HALO, k), lambda i, j: (jnp.minimum((i + 1) * r, nblk - 1), 0)),
            pl.BlockSpec((1, k), lambda i, j: (0, 0)),
            pl.BlockSpec((k, tf), lambda i, j: (0, j)),
            pl.BlockSpec((k, tf), lambda i, j: (0, j + nf)),
            pl.BlockSpec((3, tf), lambda i, j: (0, j)),
            pl.BlockSpec((1, tf), lambda i, j: (0, j)),
        ],
        out_specs=pl.BlockSpec((tm, tf), lambda i, j: (i, j)),
        out_shape=jax.ShapeDtypeStruct((m, D_FF), BF16),
        scratch_shapes=[pltpu.VMEM((tm + 2 * HALO, k), BF16)],
        compiler_params=_params("parallel", "arbitrary"),
        name="ffn_up",
    )(h, h, h, g.reshape(1, k), w_up, w_up, conv_w, conv_b.reshape(1, D_FF))


def _ple_kernel(h_ref, *refs, split, tn, p_tiles):
    p_refs = refs[:len(p_tiles)]
    g_ref, wg_ref, wp_ref, fn_ref = refs[len(p_tiles):len(p_tiles) + 4]
    out_refs = refs[len(p_tiles) + 4:]
    x = h_ref[...]
    xn = _rms(x, g_ref[...]).astype(BF16)
    pb = _part_read(p_refs, p_tiles, pl.program_id(0)).astype(BF16)
    y_ref = out_refs[-1]
    for c in range(D_MODEL // tn):
        sl = slice(c * tn, (c + 1) * tn)
        gate = _dot(xn, wg_ref[:, sl])
        y_ref[:, sl] = x[:, sl] + _dot(pb, wp_ref[:, sl]) * _sigmoid(gate)
    if split is not None:
        first_ref, second_ref, _ = out_refs
        y = _rms(y_ref[...], fn_ref[...])
        i = pl.program_id(0)

        @pl.when(i < split)
        def _():
            first_ref[...] = y

        @pl.when(i >= split)
        def _():
            second_ref[...] = y


def _ple(h, p_parts, layer, g, wg, wp, fn, m_first=None):
    m, k = h.shape
    tm = _pick(math.gcd(*[p.shape[0] // DEPTH for p in p_parts]), 512, 256, 128)
    p_tiles = tuple(p.shape[0] // DEPTH // tm for p in p_parts)
    assert sum(p_tiles) * tm == m and (m_first is None or m_first % tm == 0)
    p_specs, start = [], 0
    for nt in p_tiles:
        p_specs.append(pl.BlockSpec((tm, PLE_DIM), functools.partial(
            lambda i, s, nt: (layer * nt + jnp.clip(i - s, 0, nt - 1), 0), s=start, nt=nt)))
        start += nt
    tile = pl.BlockSpec((tm, k), lambda i: (i, 0))
    if m_first is None:
        split, scratch = None, []
        out_specs, out_shape = tile, jax.ShapeDtypeStruct((m, k), F32)
    else:
        split, scratch = m_first // tm, [pltpu.VMEM((tm, k), F32)]
        out_specs = [pl.BlockSpec((tm, k), lambda i: (jnp.minimum(i, split - 1), 0)),
                     pl.BlockSpec((tm, k), lambda i: (jnp.maximum(i - split, 0), 0))]
        out_shape = [jax.ShapeDtypeStruct((m_first, k), F32), jax.ShapeDtypeStruct((m - m_first, k), F32)]
    return pl.pallas_call(
        functools.partial(_ple_kernel, split=split, tn=512, p_tiles=p_tiles),
        grid=(m // tm,),
        in_specs=[tile] + p_specs + [
            pl.BlockSpec((1, k), lambda i: (0, 0)),
            pl.BlockSpec(wg.shape, lambda i: (0, 0)),
            pl.BlockSpec(wp.shape, lambda i: (0, 0)),
            pl.BlockSpec((1, k), lambda i: (0, 0)),
        ],
        out_specs=out_specs,
        out_shape=out_shape,
        scratch_shapes=scratch,
        compiler_params=_params("arbitrary"),
        name="ple_gate",
    )(h, *p_parts, g.reshape(1, k), wg, wp, fn.reshape(1, k))


def _rot_cols(w):
    half = ROPE_DIM // 2
    return jnp.concatenate([-w[..., half:], w[..., :half]], axis=-1)


def _pad_rope(w):
    return jnp.concatenate([w, jnp.zeros_like(w)], axis=-1)


def _prepare(attn_norm, w_in, hg_lower, hg_norm, w_q_b, w_kv_b, w_out, w_up, w_down,
             w_ple_gate, w_ple_proj):
    w_kr = w_in[:, :, OFF_KR:]
    wq = w_q_b.reshape(DEPTH, LORA, HEADS, HEAD_DIM + ROPE_DIM)
    wq_r = wq[..., HEAD_DIM:]
    wq = jnp.concatenate([wq[..., :HEAD_DIM], _pad_rope(wq_r), _pad_rope(_rot_cols(wq_r))], axis=-1)
    wkv = w_kv_b.reshape(DEPTH, LORA, HEADS, 2 * HEAD_DIM)

    lb = jnp.cumsum(jax.nn.softmax(hg_lower.astype(F32), axis=1), axis=1)
    lb = jnp.maximum(lb - lb[:, :1], 0.0)
    loglb = jnp.maximum(jnp.log(lb) * LOG2E, NEG).reshape(2, DEPTH, HEADS, HEAD_DIM)
    log1m = (jnp.log1p(-lb) * LOG2E).reshape(2, DEPTH, HEADS, HEAD_DIM)
    gain = hg_norm.astype(F32).reshape(DEPTH, HEADS, HEAD_DIM)
    zeros = jnp.zeros_like(gain)
    crow = jnp.stack([loglb[0], log1m[0], loglb[1], log1m[1], gain, zeros, zeros, zeros], axis=2)

    return dict(
        w_in=w_in[:, :, :OFF_KR].astype(BF16),
        w_kr=jnp.concatenate([_pad_rope(w_kr), _pad_rope(_rot_cols(w_kr))], axis=-1).astype(BF16),
        w_q=wq.reshape(DEPTH, LORA, HEADS * 3 * LANES).astype(BF16),
        w_k=wkv[..., :HEAD_DIM].reshape(DEPTH, LORA, HG_WIDTH).astype(BF16),
        w_vt=jnp.swapaxes(wkv[..., HEAD_DIM:].reshape(DEPTH, LORA, HG_WIDTH), 1, 2).astype(BF16),
        w_out=w_out.astype(BF16),
        w_up=w_up.astype(BF16),
        w_down=w_down.astype(BF16),
        w_ple_gate=w_ple_gate.astype(BF16),
        w_ple_proj=w_ple_proj.astype(BF16),
        crow=crow,
    )


def _rope_tables(seq):
    inv = ROPE_THETA ** (-jnp.arange(0, ROPE_DIM, 2, dtype=F32) / ROPE_DIM)
    ang = jnp.arange(seq, dtype=F32)[:, None] * inv[None, :]
    pad = jnp.zeros((seq, LANES - ROPE_DIM), F32)
    cosz = jnp.concatenate([jnp.cos(ang), jnp.cos(ang), pad], axis=-1)
    sinz = jnp.concatenate([jnp.sin(ang), jnp.sin(ang), pad], axis=-1)
    return cosz, sinz


def _trunk(xs, ps, attn_norm, q_a_norm, kv_a_norm, ffn_norm, conv_w, conv_b, ple_norm, final_norm, prm):
    seq, d = xs[0].shape[1:]
    nbs = [x.shape[0] for x in xs]
    nb = sum(nbs)
    h_parts = [x.reshape(n * seq, d) for x, n in zip(xs, nbs)]
    p_parts = [p.reshape(DEPTH * n * seq, PLE_DIM) for p, n in zip(ps, nbs)]
    cosz, sinz = _rope_tables(seq)
    tm = _pick(seq, 1024, 512, 256, 128)
    tk = _pick(seq, 1024, 512, 256, 128)
    for l in range(DEPTH):
        z, zkr = _norm_mm(h_parts, attn_norm[l], prm["w_in"][l], prm["w_kr"][l], tm,
                          1024 // len(h_parts))
        qp = _qproj(z, q_a_norm[l], prm["w_q"][l], cosz, sinz, seq)
        kp, vt = _kvproj(z, zkr, kv_a_norm[l], prm["w_k"][l], prm["w_vt"][l], cosz, sinz, nb, seq, tk)
        o_hg, o_mla = _mixer(z, prm["crow"][l], qp, kp, vt, nb, seq, tk)
        h = _mm_res([o_hg, o_mla], prm["w_out"][l], h_parts, tm, 512)
        act = _ffn_up(h, ffn_norm[l], prm["w_up"][l], conv_w[l], conv_b[l], seq)
        h = _mm_res([act], prm["w_down"][l], [h], tm, 512)
        h = _ple(h, p_parts, l, ple_norm[l], prm["w_ple_gate"][l], prm["w_ple_proj"][l], final_norm,
                 m_first=nbs[0] * seq if l == DEPTH - 1 else None)
        h_parts = [h]
    return h[0].reshape(nbs[0], seq, d), h[1].reshape(nbs[1], seq, d)


def kernel(x_prompt, x_sample, p_prompt, p_sample, attn_norm, w_in, hg_lower, hg_norm, q_a_norm,
           w_q_b, kv_a_norm, w_kv_b, w_out, ffn_norm, w_up, conv_w, conv_b, w_down, ple_norm,
           w_ple_gate, w_ple_proj, final_norm):
    assert x_prompt.shape[1:] == x_sample.shape[1:], "both groups must share the sequence length"
    prm = _prepare(attn_norm, w_in, hg_lower, hg_norm, w_q_b, w_kv_b, w_out, w_up, w_down,
                   w_ple_gate, w_ple_proj)
    return _trunk([x_prompt, x_sample], [p_prompt, p_sample], attn_norm, q_a_norm, kv_a_norm,
                  ffn_norm, conv_w, conv_b, ple_norm, final_norm, prm)
```

```python
import functools
import math

import jax
import jax.numpy as jnp
import numpy as np
from jax import lax
from jax.experimental import pallas as pl
from jax.experimental.pallas import tpu as pltpu

F32 = jnp.float32
BF16 = jnp.bfloat16

D_MODEL = 2048
DEPTH = 4
HEADS = 8
HEAD_DIM = 128
ROPE_DIM = 64
LORA = 512
HG_WIDTH = HEADS * HEAD_DIM
OFF_Q, OFF_FF, OFF_FB, OFF_I, OFF_G, OFF_QA, OFF_KVA, OFF_KR = (
    0, 1024, 2048, 3072, 4096, 5120, 5632, 6144)
D_FF = 5632
PLE_DIM = 256
EPS = 1e-6
ROPE_THETA = 10000.0
NEG = -1e30

LANES = 128
SUBLANES = 8
VMEM_LIMIT = 56 * 2**20

HG_CHUNK = 128
HG_ROWS = SUBLANES
Q_PAD = 2 * LANES
LOG2E = math.log2(math.e)
QK_SCALE = (HEAD_DIM + ROPE_DIM) ** -0.5 * LOG2E
ATT_GROUP = 2 * LANES
ATT_BLOCKS = 2
VT_ROWS = HEAD_DIM + 16
MIX_ROUNDS = 2


def _params(*sem):
    return pltpu.CompilerParams(dimension_semantics=sem, vmem_limit_bytes=VMEM_LIMIT)


def _pick(n, *cands):
    for c in cands:
        if n % c == 0:
            return c
    raise ValueError(f"no tile for {n} in {cands}")


def _rms(x, g):
    ms = jnp.mean(x * x, axis=-1, keepdims=True)
    return x * lax.rsqrt(ms + EPS) * g


def _sigmoid(x):
    return 1.0 / (1.0 + jnp.exp(-x))


def _dot(a, b):
    return jnp.dot(a, b, preferred_element_type=F32)


def _dot_nt(a, b):
    return lax.dot_general(a, b, (((1,), (1,)), ((), ())), preferred_element_type=F32)


def _dot_tn(a, b):
    return lax.dot_general(a, b, (((0,), (0,)), ((), ())), preferred_element_type=F32)


def _part_tiles(parts, tm):
    assert all(p.shape[0] % tm == 0 for p in parts)
    return tuple(p.shape[0] // tm for p in parts)


def _part_specs(parts, tm, width, col):
    specs, start = [], 0
    for nt in _part_tiles(parts, tm):
        def imap(i, *rest, start=start, nt=nt):
            inside = (i >= start) & (i < start + nt)
            return jnp.clip(i - start, 0, nt - 1), jnp.where(inside, col(i, *rest), 0)
        specs.append(pl.BlockSpec((tm, width), imap))
        start += nt
    return specs


def _part_read(refs, tiles, i):
    x, start = refs[0][...], tiles[0]
    for ref, nt in zip(refs[1:], tiles[1:]):
        x = jnp.where(i >= start, ref[...], x)
        start += nt
    return x


def _norm_mm_kernel(*refs, tiles):
    x_refs = refs[:len(tiles)]
    g_ref, w_ref, ws_ref, o_ref, os_ref, xn_ref = refs[len(tiles):]

    @pl.when(pl.program_id(1) == 0)
    def _():
        x = _part_read(x_refs, tiles, pl.program_id(0))
        xn_ref[...] = _rms(x, g_ref[...]).astype(BF16)
        os_ref[...] = _dot(xn_ref[...], ws_ref[...])

    o_ref[...] = _dot(xn_ref[...], w_ref[...])


def _norm_mm(x_parts, g, w, w_side, tm, tn):
    k, n = w.shape
    ns = w_side.shape[1]
    tiles = _part_tiles(x_parts, tm)
    m = sum(tiles) * tm
    return pl.pallas_call(
        functools.partial(_norm_mm_kernel, tiles=tiles),
        grid=(m // tm, n // tn),
        in_specs=_part_specs(x_parts, tm, k, lambda i, j: 0) + [
            pl.BlockSpec((1, k), lambda i, j: (0, 0)),
            pl.BlockSpec((k, tn), lambda i, j: (0, j)),
            pl.BlockSpec((k, ns), lambda i, j: (0, 0)),
        ],
        out_specs=[pl.BlockSpec((tm, tn), lambda i, j: (i, j)),
                   pl.BlockSpec((tm, ns), lambda i, j: (i, 0))],
        out_shape=[jax.ShapeDtypeStruct((m, n), F32), jax.ShapeDtypeStruct((m, ns), F32)],
        scratch_shapes=[pltpu.VMEM((tm, k), BF16)],
        compiler_params=_params("parallel", "arbitrary"),
        name="norm_mm",
    )(*x_parts, g.reshape(1, k), w, w_side)


def _hgrn_gates(z, loglb, log1m):
    z2 = z * LOG2E
    l1p = jnp.log2(1.0 + jnp.exp2(-jnp.abs(z2)))
    c = log1m + (jnp.minimum(z2, 0.0) - l1p)
    logf = jnp.maximum(loglb, c) + jnp.log2(1.0 + jnp.exp2(-jnp.abs(loglb - c)))
    return logf, c - z2


def _hgrn_chunk(q, zf, v, cmat, lvl, loglb, log1m, st_ref, cb_ref, reverse):
    c = q.shape[0]
    nv = c // HG_ROWS
    logf, logk = _hgrn_gates(zf, loglb, log1m)

    hi = logf.astype(BF16)
    lo = (logf - hi.astype(F32)).astype(BF16)
    yield
    bb = _dot(cmat, jnp.concatenate([hi, lo], axis=1))
    b = bb[:, :LANES] + bb[:, LANES:]

    def rows(x, p):
        i = nv - 1 - p if reverse else p
        return x[i * HG_ROWS:(i + 1) * HG_ROWS, :]

    def assemble(groups):
        return jnp.concatenate(groups[::-1] if reverse else groups, axis=0)

    edge = 0 if reverse else HG_ROWS - 1
    b_p = [rows(b, p) for p in range(nv)]
    q_p = [rows(q, p) for p in range(nv)]
    lk_p = [rows(logk, p) for p in range(nv)]
    bnd = [jnp.broadcast_to(x[edge:edge + 1, :], (HG_ROWS, LANES)) for x in b_p]
    zero = jnp.zeros((HG_ROWS, LANES), F32)

    a = jnp.zeros((c, c), F32)
    level, mv = 1, nv // 2
    while mv >= 1:
        qm = [zero] * nv
        km = [zero] * nv
        for g in range(nv // (2 * mv)):
            ref = bnd[2 * mv * g + mv - 1]
            for p in range(2 * mv * g, 2 * mv * g + mv):
                km[p] = jnp.exp2(lk_p[p] + (ref - b_p[p]))
            for p in range(2 * mv * g + mv, 2 * mv * (g + 1)):
                qm[p] = q_p[p] * jnp.exp2(b_p[p] - ref)
        yield
        al = _dot(assemble(qm).astype(BF16), assemble(km).T.astype(BF16))
        a = jnp.where(lvl == float(level), al, a)
        level += 1
        mv //= 2

    cb_ref[...] = b - logk
    slabs = []
    for s in range(HG_ROWS):
        col = []
        for p in range(nv):
            r = (nv - 1 - p if reverse else p) * HG_ROWS + s
            cs = jnp.broadcast_to(cb_ref[r:r + 1, :], (HG_ROWS, LANES))
            col.append(q_p[p] * jnp.exp2(jnp.minimum(b_p[p] - cs, 0.0)))
        slabs.append(assemble(col).astype(BF16))
        yield
    adiag = yield jnp.concatenate(slabs, axis=1)
    a = jnp.where(lvl == float(level), adiag, a)

    tot = bnd[nv - 1]
    qbar = assemble([q_p[p] * jnp.exp2(b_p[p]) for p in range(nv)]).astype(BF16)
    kbar = assemble([jnp.exp2(lk_p[p] + (tot - b_p[p])) for p in range(nv)]).astype(BF16)
    vb = v.astype(BF16)
    yield
    st = st_ref[...]
    o = _dot(jnp.concatenate([qbar, a.astype(BF16)], axis=1),
             jnp.concatenate([st.T.astype(BF16), vb], axis=0))
    st_ref[...] = jnp.exp2(tot[0:1, :]) * st + _dot_tn(vb, kbar)
    return o


def _hgrn_consts(chunk):
    t = np.arange(chunk)[:, None]
    s = np.arange(chunk)[None, :]
    cm = np.stack([(s <= t), (s >= t)]).astype(np.float32)
    nlev = int(round(math.log2(chunk // HG_ROWS)))
    lv = np.zeros((chunk, chunk), np.float32)
    for level in range(1, nlev + 1):
        m = chunk >> level
        hit = (t // (2 * m) == s // (2 * m)) & (t // m != s // m) & (s < t)
        lv[hit] = level
    lv[((t // HG_ROWS) == (s // HG_ROWS)) & (s <= t)] = nlev + 1
    lvl = np.stack([lv, lv.T]).astype(np.float32)
    em = (np.arange(HG_ROWS * LANES)[:, None] // LANES == (np.arange(chunk)[None, :] % HG_ROWS))
    return jnp.asarray(cm, BF16), jnp.asarray(lvl, F32), jnp.asarray(em.astype(np.float32), BF16)


def _qproj_kernel(z_ref, g_ref, w_ref, cos_ref, sin_ref, o_ref):
    xn = _rms(z_ref[...], g_ref[...]).astype(BF16)
    cz = cos_ref[...]
    sz = sin_ref[...]
    for h in range(HEADS):
        acc = _dot(xn, w_ref[:, h * 3 * LANES:(h + 1) * 3 * LANES])
        o_ref[:, h * Q_PAD:h * Q_PAD + LANES] = (acc[:, :LANES] * QK_SCALE).astype(BF16)
        rope = acc[:, LANES:2 * LANES] * cz + acc[:, 2 * LANES:] * sz
        o_ref[:, h * Q_PAD + LANES:(h + 1) * Q_PAD] = (rope * QK_SCALE).astype(BF16)


def _kvproj_kernel(z_ref, kr_ref, g_ref, wk_ref, wvt_ref, cos_ref, sin_ref, k_ref, vt_ref):
    xn = _rms(z_ref[...], g_ref[...]).astype(BF16)
    kr = kr_ref[...]
    krope = (kr[:, :LANES] * cos_ref[...] + kr[:, LANES:] * sin_ref[...]).astype(BF16)
    kn = _dot(xn, wk_ref[...])
    for h in range(HEADS):
        k_ref[:, h * Q_PAD:h * Q_PAD + LANES] = kn[:, h * LANES:(h + 1) * LANES].astype(BF16)
        k_ref[:, h * Q_PAD + LANES:(h + 1) * Q_PAD] = krope
    vt = _dot_nt(wvt_ref[...], xn).astype(BF16)
    ones = jnp.ones((VT_ROWS - HEAD_DIM, vt.shape[1]), BF16)
    for h in range(HEADS):
        vt_ref[0, 0, h, :HEAD_DIM, :] = vt[h * HEAD_DIM:(h + 1) * HEAD_DIM, :]
        vt_ref[0, 0, h, HEAD_DIM:, :] = ones


def _mla_proj_kernel(z_ref, kr_ref, gq_ref, gkv_ref, wq_ref, wk_ref, wvt_ref, cos_ref, sin_ref,
                     q_ref, k_ref, vt_ref):
    _qproj_kernel(z_ref.at[:, :LORA], gq_ref, wq_ref, cos_ref, sin_ref, q_ref)
    _kvproj_kernel(z_ref.at[:, LORA:], kr_ref, gkv_ref, wk_ref, wvt_ref, cos_ref, sin_ref, k_ref, vt_ref)


def _mla_proj(z, zkr, gq, gkv, wq, wk, wvt, cosz, sinz, nb, seq, tk):
    assert OFF_KVA == OFF_QA + LORA and OFF_QA % (2 * LORA) == 0
    m = z.shape[0]
    ns = seq // tk
    full = lambda a: pl.BlockSpec(a.shape, lambda i: (0, 0))
    return pl.pallas_call(
        _mla_proj_kernel,
        grid=(m // tk,),
        in_specs=[
            pl.BlockSpec((tk, 2 * LORA), lambda i: (i, OFF_QA // (2 * LORA))),
            pl.BlockSpec((tk, 2 * LANES), lambda i: (i, 0)),
            pl.BlockSpec((1, LORA), lambda i: (0, 0)),
            pl.BlockSpec((1, LORA), lambda i: (0, 0)),
            full(wq), full(wk), full(wvt),
            pl.BlockSpec((tk, LANES), lambda i: (i % ns, 0)),
            pl.BlockSpec((tk, LANES), lambda i: (i % ns, 0)),
        ],
        out_specs=[
            pl.BlockSpec((tk, HEADS * Q_PAD), lambda i: (i, 0)),
            pl.BlockSpec((tk, HEADS * Q_PAD), lambda i: (i, 0)),
            pl.BlockSpec((1, 1, HEADS, VT_ROWS, tk), lambda i: (i // ns, i % ns, 0, 0, 0)),
        ],
        out_shape=[
            jax.ShapeDtypeStruct((m, HEADS * Q_PAD), BF16),
            jax.ShapeDtypeStruct((m, HEADS * Q_PAD), BF16),
            jax.ShapeDtypeStruct((nb, ns, HEADS, VT_ROWS, tk), BF16),
        ],
        compiler_params=_params("parallel"),
        name="mla_proj",
    )(z, zkr, gq.reshape(1, LORA), gkv.reshape(1, LORA), wq, wk, wvt, cosz, sinz)


def _col_reduce(x, op, slab=64):
    r, c = x.shape
    if r > slab and r % slab == 0:
        x = op(x.reshape(r // slab, slab, c), axis=0)
    return op(x, axis=0, keepdims=True)


def _interleave(units, gens, rounds, emat):
    results = [None] * len(gens)
    inbox = [None] * len(gens)
    live = list(range(len(gens)))

    def advance():
        asked = []
        for idx in list(live):
            try:
                out = gens[idx].send(inbox[idx])
            except StopIteration as done:
                results[idx] = done.value
                live.remove(idx)
                continue
            inbox[idx] = None
            if out is not None:
                asked.append((idx, out))
        if asked:
            prod = _dot(jnp.concatenate([x for _, x in asked], axis=0), emat)
            row = 0
            for idx, x in asked:
                inbox[idx] = prod[row:row + x.shape[0], :]
                row += x.shape[0]

    for unit in units:
        unit()
        for _ in range(rounds):
            advance()
    while live:
        advance()
    return results


def _mixer_kernel(qf_ref, zf_ref, vf_ref, gf_ref, qb_ref, zb_ref, vb_ref, gb_ref,
                  crow_ref, cmat_ref, lvl_ref, emat_ref, q_ref, k_ref, vt_ref,
                  ohg_ref, oat_ref,
                  part_ref, stf_ref, stb_ref, cb_ref, acc_ref, st_ref, mx_ref,
                  *, chunk, tk, nk, nblk):
    c = pl.program_id(2)
    nsteps = pl.num_programs(2)
    tq = q_ref.shape[0]
    nslot, ng, _, gw = st_ref.shape
    ntrips = nk // nblk
    nch = tq // chunk
    assert nblk % nslot == 0 and nk % nblk == 0 and nch % ntrips == 0
    hpt = nch // ntrips

    @pl.when(c == 0)
    def _():
        stf_ref[...] = jnp.zeros_like(stf_ref)
        stb_ref[...] = jnp.zeros_like(stb_ref)
        part_ref[...] = jnp.zeros_like(part_ref)

    acc_ref[...] = jnp.zeros_like(acc_ref)
    crow = crow_ref[0]
    gain = crow[4:5, :]
    emat = emat_ref[...]

    def emit(o, row, g):
        t = o + part_ref[pl.ds(row, chunk), :]
        part_ref[pl.ds(row, chunk), :] = t
        ohg_ref[pl.ds(row, chunk), :] = (_rms(t, gain) * (g * _sigmoid(g))).astype(BF16)

    def fwd(j, u):
        rf = pl.multiple_of(j * chunk, chunk)
        sl = pl.ds(rf, chunk)
        gen = _hgrn_chunk(qf_ref[sl, :], zf_ref[sl, :], vf_ref[sl, :], cmat_ref[0], lvl_ref[0],
                          crow[0:1, :], crow[1:2, :], stf_ref, cb_ref.at[0, u], False)
        return gen, pl.multiple_of(c * tq + rf, chunk), gf_ref, sl

    def bwd(j, u):
        rb = pl.multiple_of((nch - 1 - j) * chunk, chunk)
        sl = pl.ds(rb, chunk)
        gen = _hgrn_chunk(qb_ref[sl, :], zb_ref[sl, :], vb_ref[sl, :], cmat_ref[1], lvl_ref[1],
                          crow[2:3, :], crow[3:4, :], stb_ref, cb_ref.at[1, u], True)
        return gen, pl.multiple_of((nsteps - 1 - c) * tq + rb, chunk), gb_ref, sl

    def scores(j, slot, g):
        kb = k_ref[pl.ds(pl.multiple_of(j * tk, tk), tk), :]
        st = _dot_nt(kb, q_ref[g * gw:(g + 1) * gw, :])
        st_ref[slot, g] = st
        mx_ref[slot, g] = _col_reduce(st, jnp.max)

    def consume(j, slot, g, m):
        cols = slice(g * gw, (g + 1) * gw)
        m_new = jnp.maximum(m, mx_ref[slot, g])
        alpha = jnp.exp2(m - m_new)
        p = jnp.exp2((st_ref[slot, g] - m_new).astype(BF16))
        acc_ref[:, cols] = alpha * acc_ref[:, cols] + _dot(vt_ref[0, j, 0], p)
        return m_new

    def trip(t, ms, last):
        ms = list(ms)
        j = nblk * t
        units = []
        for u in range(nblk):
            for g in range(ng):
                def unit(u=u, g=g):
                    if not (last and u == nblk - 1):
                        scores(j + u + 1, (u + 1) % nslot, g)
                    ms[g] = consume(j + u, u % nslot, g, ms[g])
                units.append(unit)
        chains = [f(t * hpt + v, v) for v in range(hpt) for f in (fwd, bwd)]
        outs = _interleave(units, [ch[0] for ch in chains], MIX_ROUNDS, emat)
        for o, (_, row, g_ref, sl) in zip(outs, chains):
            emit(o, row, g_ref[sl, :])
        return tuple(ms)

    for g in range(ng):
        scores(0, 0, g)
    init = tuple(jnp.full((1, gw), NEG, F32) for _ in range(ng))
    ms = lax.fori_loop(0, ntrips - 1, lambda t, ms: trip(t, ms, False), init)
    trip(ntrips - 1, ms, True)
    for g in range(ng):
        cols = slice(g * gw, (g + 1) * gw)
        oat_ref[cols, :] = (acc_ref[:HEAD_DIM, cols] / acc_ref[HEAD_DIM:HEAD_DIM + 1, cols]).T.astype(BF16)


def _mixer(z, crow, qp, kp, vt, nb, seq, tk):
    m = z.shape[0]
    tq = _pick(seq, 1024, 512, 256, 128)
    n = seq // tq
    assert n % 2 == 0, "the two scan directions meet in the middle"
    half = n // 2
    nk = seq // tk
    nblk = ATT_BLOCKS if nk % ATT_BLOCKS == 0 else 2
    gw = min(ATT_GROUP, tq)
    chunk = HG_CHUNK
    hpt = tq // chunk // (nk // nblk)
    cmat, lvl, emat = _hgrn_consts(chunk)

    def zspec(col, fwd, gate=False):
        if fwd:
            blk = (lambda c: jnp.maximum(c, half)) if gate else (lambda c: c)
        else:
            blk = (lambda c: jnp.minimum(n - 1 - c, half - 1)) if gate else (lambda c: n - 1 - c)
        return pl.BlockSpec((tq, LANES), lambda b, h, c: (b * n + blk(c), col // LANES + h))

    const3 = lambda shape: pl.BlockSpec(shape, lambda b, h, c: (0, 0, 0))
    return pl.pallas_call(
        functools.partial(_mixer_kernel, chunk=chunk, tk=tk, nk=nk, nblk=nblk),
        grid=(nb, HEADS, n),
        in_specs=[
            zspec(OFF_Q, True), zspec(OFF_FF, True), zspec(OFF_I, True), zspec(OFF_G, True, True),
            zspec(OFF_Q, False), zspec(OFF_FB, False), zspec(OFF_I, False), zspec(OFF_G, False, True),
            pl.BlockSpec((1, SUBLANES, LANES), lambda b, h, c: (h, 0, 0)),
            const3(cmat.shape), const3(lvl.shape),
            pl.BlockSpec(emat.shape, lambda b, h, c: (0, 0)),
            pl.BlockSpec((tq, Q_PAD), lambda b, h, c: (b * n + c, h)),
            pl.BlockSpec((seq, Q_PAD), lambda b, h, c: (b, h)),
            pl.BlockSpec((1, nk, 1, VT_ROWS, tk), lambda b, h, c: (b, 0, h, 0, 0)),
        ],
        out_specs=[pl.BlockSpec((seq, LANES), lambda b, h, c: (b, h)),
                   pl.BlockSpec((tq, HEAD_DIM), lambda b, h, c: (b * n + c, h))],
        out_shape=[jax.ShapeDtypeStruct((m, HG_WIDTH), BF16), jax.ShapeDtypeStruct((m, HG_WIDTH), BF16)],
        scratch_shapes=[
            pltpu.VMEM((seq, LANES), F32),
            pltpu.VMEM((HEAD_DIM, HEAD_DIM), F32),
            pltpu.VMEM((HEAD_DIM, HEAD_DIM), F32),
            pltpu.VMEM((2, hpt, chunk, LANES), F32),
            pltpu.VMEM((VT_ROWS, tq), F32),
            pltpu.VMEM((2, tq // gw, tk, gw), F32),
            pltpu.VMEM((2, tq // gw, 1, gw), F32),
        ],
        compiler_params=_params("parallel", "parallel", "arbitrary"),
        name="token_mixer",
    )(z, z, z, z, z, z, z, z, crow, cmat, lvl, emat, qp, kp, vt)


def _mm_res_kernel(*refs, n_in, tiles):
    res_refs, o_ref = refs[2 * n_in:-1], refs[-1]
    acc = _part_read(res_refs, tiles, pl.program_id(0))
    for a_ref, w_ref in zip(refs[:n_in], refs[n_in:2 * n_in]):
        acc = acc + _dot(a_ref[...], w_ref[...])
    o_ref[...] = acc


def _mm_res(a_list, w, res_parts, tm, tn):
    n = w.shape[1]
    tiles = _part_tiles(res_parts, tm)
    m = sum(tiles) * tm
    n_in = len(a_list)
    kb = a_list[0].shape[1]
    assert all(a.shape == (m, kb) for a in a_list) and w.shape[0] == n_in * kb
    w_list = [w] * n_in
    in_specs = [pl.BlockSpec((tm, kb), lambda i, j: (i, 0)) for _ in a_list]
    in_specs += [pl.BlockSpec((kb, tn), functools.partial(lambda i, j, r: (r, j), r=r))
                 for r in range(n_in)]
    in_specs += _part_specs(res_parts, tm, tn, lambda i, j: j)
    return pl.pallas_call(
        functools.partial(_mm_res_kernel, n_in=n_in, tiles=tiles),
        grid=(m // tm, n // tn),
        in_specs=in_specs,
        out_specs=pl.BlockSpec((tm, tn), lambda i, j: (i, j)),
        out_shape=jax.ShapeDtypeStruct((m, n), F32),
        compiler_params=_params("parallel", "arbitrary"),
        name="mm_residual",
    )(*a_list, *w_list, *res_parts)


HALO = 16


def _ffn_up_kernel(x_ref, xp_ref, xn_ref, g_ref, wg_ref, wu_ref, cw_ref, cb_ref, o_ref, n_ref,
                   *, tm, tiles_per_seq):
    i = pl.program_id(0)

    @pl.when(pl.program_id(1) == 0)
    def _():
        g = g_ref[...]
        first = (i % tiles_per_seq) == 0
        last = (i % tiles_per_seq) == tiles_per_seq - 1
        n_ref[0:HALO, :] = jnp.where(first, 0.0, _rms(xp_ref[...], g)).astype(BF16)
        n_ref[HALO:HALO + tm, :] = _rms(x_ref[...], g).astype(BF16)
        n_ref[HALO + tm:, :] = jnp.where(last, 0.0, _rms(xn_ref[...], g)).astype(BF16)

    ge = _dot(n_ref[...], wg_ref[...])
    rows = tm + 2 * HALO
    prev = pltpu.roll(ge, 1, 0)[HALO:HALO + tm, :]
    nxt = pltpu.roll(ge, rows - 1, 0)[HALO:HALO + tm, :]
    cw = cw_ref[...]
    gate = prev * cw[0:1, :] + ge[HALO:HALO + tm, :] * cw[1:2, :] + nxt * cw[2:3, :] + cb_ref[...]
    up = _dot(n_ref[HALO:HALO + tm, :], wu_ref[...])
    o_ref[...] = (gate * _sigmoid(gate) * up).astype(BF16)


def _ffn_up(h, g, w_up, conv_w, conv_b, seq):
    m, k = h.shape
    tm = _pick(seq, 1024, 512, 256, 128)
    tf = 512
    nf = D_FF // tf
    tps = seq // tm
    r = tm // HALO
    nblk = m // HALO
    return pl.pallas_call(
        functools.partial(_ffn_up_kernel, tm=tm, tiles_per_seq=tps),
        grid=(m // tm, nf),
        in_specs=[
            pl.BlockSpec((tm, k), lambda i, j: (i, 0)),
            pl.BlockSpec((HALO, k), lambda i, j: (jnp.maximum(i * r - 1, 0), 0)),
            pl.BlockSpec((HALO, k), lambda i, j: (jnp.minimum((i + 1) * r, nblk - 1), 0)),
            pl.BlockSpec((1, k), lambda i, j: (0, 0)),
            pl.BlockSpec((k, tf), lambda i, j: (0, j)),
            pl.BlockSpec((k, tf), lambda i, j: (0, j + nf)),
            pl.BlockSpec((3, tf), lambda i, j: (0, j)),
            pl.BlockSpec((1, tf), lambda i, j: (0, j)),
        ],
        out_specs=pl.BlockSpec((tm, tf), lambda i, j: (i, j)),
        out_shape=jax.ShapeDtypeStruct((m, D_FF), BF16),
        scratch_shapes=[pltpu.VMEM((tm + 2 * HALO, k), BF16)],
        compiler_params=_params("parallel", "arbitrary"),
        name="ffn_up",
    )(h, h, h, g.reshape(1, k), w_up, w_up, conv_w, conv_b.reshape(1, D_FF))


def _ple_kernel(h_ref, *refs, split, tn, p_tiles):
    p_refs = refs[:len(p_tiles)]
    g_ref, wg_ref, wp_ref, fn_ref = refs[len(p_tiles):len(p_tiles) + 4]
    out_refs = refs[len(p_tiles) + 4:]
    x = h_ref[...]
    xn = _rms(x, g_ref[...]).astype(BF16)
    pb = _part_read(p_refs, p_tiles, pl.program_id(0)).astype(BF16)
    y_ref = out_refs[-1]
    for c in range(D_MODEL // tn):
        sl = slice(c * tn, (c + 1) * tn)
        gate = _dot(xn, wg_ref[:, sl])
        y_ref[:, sl] = x[:, sl] + _dot(pb, wp_ref[:, sl]) * _sigmoid(gate)
    if split is not None:
        first_ref, second_ref, _ = out_refs
        y = _rms(y_ref[...], fn_ref[...])
        i = pl.program_id(0)

        @pl.when(i < split)
        def _():
            first_ref[...] = y

        @pl.when(i >= split)
        def _():
            second_ref[...] = y


def _ple(h, p_parts, layer, g, wg, wp, fn, m_first=None):
    m, k = h.shape
    tm = _pick(math.gcd(*[p.shape[0] // DEPTH for p in p_parts]), 512, 256, 128)
    p_tiles = tuple(p.shape[0] // DEPTH // tm for p in p_parts)
    assert sum(p_tiles) * tm == m and (m_first is None or m_first % tm == 0)
    p_specs, start = [], 0
    for nt in p_tiles:
        p_specs.append(pl.BlockSpec((tm, PLE_DIM), functools.partial(
            lambda i, s, nt: (layer * nt + jnp.clip(i - s, 0, nt - 1), 0), s=start, nt=nt)))
        start += nt
    tile = pl.BlockSpec((tm, k), lambda i: (i, 0))
    if m_first is None:
        split, scratch = None, []
        out_specs, out_shape = tile, jax.ShapeDtypeStruct((m, k), F32)
    else:
        split, scratch = m_first // tm, [pltpu.VMEM((tm, k), F32)]
        out_specs = [pl.BlockSpec((tm, k), lambda i: (jnp.minimum(i, split - 1), 0)),
                     pl.BlockSpec((tm, k), lambda i: (jnp.maximum(i - split, 0), 0))]
        out_shape = [jax.ShapeDtypeStruct((m_first, k), F32), jax.ShapeDtypeStruct((m - m_first, k), F32)]
    return pl.pallas_call(
        functools.partial(_ple_kernel, split=split, tn=512, p_tiles=p_tiles),
        grid=(m // tm,),
        in_specs=[tile] + p_specs + [
            pl.BlockSpec((1, k), lambda i: (0, 0)),
            pl.BlockSpec(wg.shape, lambda i: (0, 0)),
            pl.BlockSpec(wp.shape, lambda i: (0, 0)),
            pl.BlockSpec((1, k), lambda i: (0, 0)),
        ],
        out_specs=out_specs,
        out_shape=out_shape,
        scratch_shapes=scratch,
        compiler_params=_params("arbitrary"),
        name="ple_gate",
    )(h, *p_parts, g.reshape(1, k), wg, wp, fn.reshape(1, k))


def _rot_cols(w):
    half = ROPE_DIM // 2
    return jnp.concatenate([-w[..., half:], w[..., :half]], axis=-1)


def _pad_rope(w):
    return jnp.concatenate([w, jnp.zeros_like(w)], axis=-1)


def _prepare(attn_norm, w_in, hg_lower, hg_norm, w_q_b, w_kv_b, w_out, w_up, w_down,
             w_ple_gate, w_ple_proj):
    w_kr = w_in[:, :, OFF_KR:]
    wq = w_q_b.reshape(DEPTH, LORA, HEADS, HEAD_DIM + ROPE_DIM)
    wq_r = wq[..., HEAD_DIM:]
    wq = jnp.concatenate([wq[..., :HEAD_DIM], _pad_rope(wq_r), _pad_rope(_rot_cols(wq_r))], axis=-1)
    wkv = w_kv_b.reshape(DEPTH, LORA, HEADS, 2 * HEAD_DIM)

    lb = jnp.cumsum(jax.nn.softmax(hg_lower.astype(F32), axis=1), axis=1)
    lb = jnp.maximum(lb - lb[:, :1], 0.0)
    loglb = jnp.maximum(jnp.log(lb) * LOG2E, NEG).reshape(2, DEPTH, HEADS, HEAD_DIM)
    log1m = (jnp.log1p(-lb) * LOG2E).reshape(2, DEPTH, HEADS, HEAD_DIM)
    gain = hg_norm.astype(F32).reshape(DEPTH, HEADS, HEAD_DIM)
    zeros = jnp.zeros_like(gain)
    crow = jnp.stack([loglb[0], log1m[0], loglb[1], log1m[1], gain, zeros, zeros, zeros], axis=2)

    return dict(
        w_in=w_in[:, :, :OFF_KR].astype(BF16),
        w_kr=jnp.concatenate([_pad_rope(w_kr), _pad_rope(_rot_cols(w_kr))], axis=-1).astype(BF16),
        w_q=wq.reshape(DEPTH, LORA, HEADS * 3 * LANES).astype(BF16),
        w_k=wkv[..., :HEAD_DIM].reshape(DEPTH, LORA, HG_WIDTH).astype(BF16),
        w_vt=jnp.swapaxes(wkv[..., HEAD_DIM:].reshape(DEPTH, LORA, HG_WIDTH), 1, 2).astype(BF16),
        w_out=w_out.astype(BF16),
        w_up=w_up.astype(BF16),
        w_down=w_down.astype(BF16),
        w_ple_gate=w_ple_gate.astype(BF16),
        w_ple_proj=w_ple_proj.astype(BF16),
        crow=crow,
    )


def _rope_tables(seq):
    inv = ROPE_THETA ** (-jnp.arange(0, ROPE_DIM, 2, dtype=F32) / ROPE_DIM)
    ang = jnp.arange(seq, dtype=F32)[:, None] * inv[None, :]
    pad = jnp.zeros((seq, LANES - ROPE_DIM), F32)
    cosz = jnp.concatenate([jnp.cos(ang), jnp.cos(ang), pad], axis=-1)
    sinz = jnp.concatenate([jnp.sin(ang), jnp.sin(ang), pad], axis=-1)
    return cosz, sinz


def _trunk(xs, ps, attn_norm, q_a_norm, kv_a_norm, ffn_norm, conv_w, conv_b, ple_norm, final_norm, prm):
    seq, d = xs[0].shape[1:]
    nbs = [x.shape[0] for x in xs]
    nb = sum(nbs)
    h_parts = [x.reshape(n * seq, d) for x, n in zip(xs, nbs)]
    p_parts = [p.reshape(DEPTH * n * seq, PLE_DIM) for p, n in zip(ps, nbs)]
    cosz, sinz = _rope_tables(seq)
    tm = _pick(seq, 1024, 512, 256, 128)
    tk = _pick(seq, 1024, 512, 256, 128)
    for l in range(DEPTH):
        z, zkr = _norm_mm(h_parts, attn_norm[l], prm["w_in"][l], prm["w_kr"][l], tm,
                          1024 // len(h_parts))
        qp, kp, vt = _mla_proj(z, zkr, q_a_norm[l], kv_a_norm[l], prm["w_q"][l], prm["w_k"][l],
                               prm["w_vt"][l], cosz, sinz, nb, seq, tk)
        o_hg, o_mla = _mixer(z, prm["crow"][l], qp, kp, vt, nb, seq, tk)
        h = _mm_res([o_hg, o_mla], prm["w_out"][l], h_parts, tm, 512)
        act = _ffn_up(h, ffn_norm[l], prm["w_up"][l], conv_w[l], conv_b[l], seq)
        h = _mm_res([act], prm["w_down"][l], [h], tm, 512)
        h = _ple(h, p_parts, l, ple_norm[l], prm["w_ple_gate"][l], prm["w_ple_proj"][l], final_norm,
                 m_first=nbs[0] * seq if l == DEPTH - 1 else None)
        h_parts = [h]
    return h[0].reshape(nbs[0], seq, d), h[1].reshape(nbs[1], seq, d)


def kernel(x_prompt, x_sample, p_prompt, p_sample, attn_norm, w_in, hg_lower, hg_norm, q_a_norm,
           w_q_b, kv_a_norm, w_kv_b, w_out, ffn_norm, w_up, conv_w, conv_b, w_down, ple_norm,
           w_ple_gate, w_ple_proj, final_norm):
    assert x_prompt.shape[1:] == x_sample.shape[1:], "both groups must share the sequence length"
    prm = _prepare(attn_norm, w_in, hg_lower, hg_norm, w_q_b, w_kv_b, w_out, w_up, w_down,
                   w_ple_gate, w_ple_proj)
    return _trunk([x_prompt, x_sample], [p_prompt, p_sample], attn_norm, q_a_norm, kv_a_norm,
                  ffn_norm, conv_w, conv_b, ple_norm, final_norm, prm)
```
